```python
import math
import jax, jax.numpy as jnp
from jax import lax
import numpy as np

D_MODEL = 1024
BATCH = 8
SEQ = 2048
DEPTH = 2

N_MIXERS = 2
N_META = 16
RMS_EPS = 1e-6

RET_HEADS = 4
RET_QK_DIM = D_MODEL // RET_HEADS
RET_V_DIM = 2 * D_MODEL // RET_HEADS
RET_CHUNK = 128
ROPE_BASE = 10000.0

LRU_WIDTH = D_MODEL
LRU_BLOCKS = 4
LRU_BLOCK_W = LRU_WIDTH // LRU_BLOCKS
LRU_C = 8.0
LRU_CONV_W = 4
LRU_CONV_LEFT = 2

FFN_HIDDEN = 3 * D_MODEL
FFN_CONV_W = 3
FFN_CONV_LEFT = 1

N_RET = (DEPTH + 1) // 2
N_LRU = DEPTH // 2

kernel_name = "hybrid_retention_rglru_convffn_encoder"


def rmsnorm(x, g):
    xf = x.astype(jnp.float32)
    y = xf * lax.rsqrt(jnp.mean(xf * xf, axis=-1, keepdims=True) + RMS_EPS)
    return (y * g.astype(jnp.float32)).astype(x.dtype)


def dwconv(x, w, b, left):
    K, C = w.shape
    y = lax.conv_general_dilated(
        x, w.astype(x.dtype)[:, None, :], window_strides=(1,),
        padding=[(left, K - 1 - left)], dimension_numbers=("NWC", "WIO", "NWC"),
        feature_group_count=C)
    return y + b.astype(x.dtype)


def rotary(x, pos):
    d = x.shape[-1]
    inv = ROPE_BASE ** (-jnp.arange(0, d, 2, dtype=jnp.float32) / d)
    ang = pos[:, None] * inv[None, :]
    cos = jnp.cos(ang)[None, :, None, :]
    sin = jnp.sin(ang)[None, :, None, :]
    x1, x2 = x[..., : d // 2], x[..., d // 2:]
    return jnp.concatenate([x1 * cos - x2 * sin, x2 * cos + x1 * sin], axis=-1)


def retention_one_direction(q, k, v, log_gamma, include_diag):
    B, H, Tp, dk = q.shape
    dv = v.shape[-1]
    C = RET_CHUNK
    N = Tp // C
    idx = jnp.arange(C, dtype=jnp.float32)
    diff = idx[:, None] - idx[None, :]
    mask = diff >= 0 if include_diag else diff > 0
    lg = log_gamma[:, None, None]
    intra_decay = jnp.where(mask[None], jnp.exp(lg * jnp.maximum(diff, 0.0)[None]), 0.0)
    q_decay = jnp.exp(log_gamma[:, None] * (idx + 1.0)[None])[..., None]
    k_decay = jnp.exp(log_gamma[:, None] * (C - 1.0 - idx)[None])[..., None]
    chunk_decay = jnp.exp(log_gamma * C)[:, None, None]

    def to_chunks(t):
        return t.reshape(B, H, N, C, t.shape[-1]).transpose(2, 0, 1, 3, 4)

    def body(R, inp):
        qc, kc, vc = inp
        s = jnp.einsum("bhid,bhjd->bhij", qc, kc) * intra_decay
        o = jnp.einsum("bhij,bhje->bhie", s, vc) + jnp.einsum("bhid,bhde->bhie", qc, R) * q_decay
        R = chunk_decay * R + jnp.einsum("bhjd,bhje->bhde", kc * k_decay, vc)
        return R, o

    R0 = jnp.zeros((B, H, dk, dv), jnp.float32)
    _, o = lax.scan(body, R0, (to_chunks(q), to_chunks(k), to_chunks(v)))
    return o.transpose(1, 2, 0, 3, 4).reshape(B, H, Tp, dv)


def retention_mixer(h, w_in, w_out, pos):
    B, T, _ = h.shape
    proj = h @ w_in
    q, k, v, g = jnp.split(proj, [D_MODEL, 2 * D_MODEL, 4 * D_MODEL], axis=-1)
    q = rotary(q.astype(jnp.float32).reshape(B, T, RET_HEADS, RET_QK_DIM), pos)
    k = rotary(k.astype(jnp.float32).reshape(B, T, RET_HEADS, RET_QK_DIM), pos) * (RET_QK_DIM ** -0.5)
    v = v.astype(jnp.float32).reshape(B, T, RET_HEADS, RET_V_DIM)
    pad = (-T) % RET_CHUNK
    def prep(t):
        return jnp.pad(t.transpose(0, 2, 1, 3), ((0, 0), (0, 0), (pad, 0), (0, 0)))
    q, k, v = prep(q), prep(k), prep(v)
    log_gamma = jnp.log1p(-jnp.exp2(-5.0 - jnp.arange(RET_HEADS, dtype=jnp.float32)))
    fwd = retention_one_direction(q, k, v, log_gamma, True)
    flip = lambda t: jnp.flip(t, axis=2)
    bwd = flip(retention_one_direction(flip(q), flip(k), flip(v), log_gamma, False))
    o = (fwd + bwd)[:, :, pad:].transpose(0, 2, 1, 3)
    mu = jnp.mean(o, axis=-1, keepdims=True)
    var = jnp.var(o, axis=-1, keepdims=True)
    o = ((o - mu) * lax.rsqrt(var + RMS_EPS)).reshape(B, T, RET_HEADS * RET_V_DIM)
    y = jax.nn.silu(g.astype(jnp.float32)) * o
    return y.astype(h.dtype) @ w_out


def rglru_one_direction(x, w_a, b_a, w_i, b_i, lam, reverse):
    B, T, R = x.shape
    xb = x.reshape(B, T, LRU_BLOCKS, LRU_BLOCK_W)
    ga = jnp.einsum("btnd,nde->btne", xb, w_a.astype(jnp.float32)).reshape(B, T, R) + b_a.astype(jnp.float32)
    gi = jnp.einsum("btnd,nde->btne", xb, w_i.astype(jnp.float32)).reshape(B, T, R) + b_i.astype(jnp.float32)
    r = jax.nn.sigmoid(ga)
    i = jax.nn.sigmoid(gi)
    log_a = -LRU_C * r * jax.nn.softplus(-lam.astype(jnp.float32))
    a = jnp.exp(log_a)
    u = jnp.sqrt(-jnp.expm1(2.0 * log_a)) * (i * x)

    def combine(e1, e2):
        a1, b1 = e1
        a2, b2 = e2
        return a1 * a2, a2 * b1 + b2

    _, hs = lax.associative_scan(combine, (a, u), reverse=reverse, axis=1)
    return hs


def rglru_mixer(h, w_in, conv_w, conv_b, w_a, b_a, w_i, b_i, lam, w_out):
    proj = h @ w_in
    gate, xr = jnp.split(proj, 2, axis=-1)
    xr = dwconv(xr, conv_w, conv_b, LRU_CONV_LEFT).astype(jnp.float32)
    y = (rglru_one_direction(xr, w_a[0], b_a[0], w_i[0], b_i[0], lam[0], False)
         + rglru_one_direction(xr, w_a[1], b_a[1], w_i[1], b_i[1], lam[1], True))
    y = y * jax.nn.gelu(gate.astype(jnp.float32), approximate=True)
    return y.astype(h.dtype) @ w_out


def conv_ffn(h, w_in, conv_w, conv_b, w_out):
    u = dwconv(h @ w_in, conv_w, conv_b, FFN_CONV_LEFT)
    val, gate = jnp.split(u, 2, axis=-1)
    return (jax.nn.gelu(gate, approximate=True) * val) @ w_out


def setup_inputs(seed: int = 0) -> dict:
    key = jax.random.key(seed)
    ks = jax.random.split(key, 20)
    D, R, F, NB, bw = D_MODEL, LRU_WIDTH, FFN_HIDDEN, LRU_BLOCKS, LRU_BLOCK_W
    nrm = jax.random.normal
    a0 = jax.random.uniform(ks[12], (N_LRU, 2, R), jnp.float32, 0.9, 0.999)
    p = a0 ** (1.0 / LRU_C)
    lam = jnp.log(p) - jnp.log1p(-p)
    return {
        "x": nrm(ks[0], (BATCH, SEQ, D), jnp.float32),
        "meta_tokens": nrm(ks[1], (N_META, D), jnp.float32),
        "norm_gains": 1.0 + 0.05 * nrm(ks[2], (DEPTH, 4, D), jnp.float32),
        "ret_w_in": nrm(ks[3], (N_RET, D, 6 * D), jnp.float32) * D ** -0.5,
        "ret_w_out": nrm(ks[4], (N_RET, 2 * D, D), jnp.float32) * (2 * D) ** -0.5,
        "lru_w_in": nrm(ks[5], (N_LRU, D, 2 * R), jnp.float32) * D ** -0.5,
        "lru_conv_w": nrm(ks[6], (N_LRU, LRU_CONV_W, R), jnp.float32) * LRU_CONV_W ** -0.5,
        "lru_conv_b": 0.01 * nrm(ks[7], (N_LRU, R), jnp.float32),
        "lru_w_a": nrm(ks[8], (N_LRU, 2, NB, bw, bw), jnp.float32) * bw ** -0.5,
        "lru_b_a": 0.01 * nrm(ks[9], (N_LRU, 2, R), jnp.float32),
        "lru_w_i": nrm(ks[10], (N_LRU, 2, NB, bw, bw), jnp.float32) * bw ** -0.5,
        "lru_b_i": 0.01 * nrm(ks[11], (N_LRU, 2, R), jnp.float32),
        "lru_lambda": lam,
        "lru_w_out": nrm(ks[13], (N_LRU, R, D), jnp.float32) * R ** -0.5,
        "ffn_w_in": nrm(ks[14], (DEPTH, D, 2 * F), jnp.float32) * D ** -0.5,
        "ffn_conv_w": nrm(ks[15], (DEPTH, FFN_CONV_W, 2 * F), jnp.float32) * FFN_CONV_W ** -0.5,
        "ffn_conv_b": 0.01 * nrm(ks[16], (DEPTH, 2 * F), jnp.float32),
        "ffn_w_out": nrm(ks[17], (DEPTH, F, D), jnp.float32) * F ** -0.5,
    }


def reference(x, meta_tokens, norm_gains, ret_w_in, ret_w_out, lru_w_in, lru_conv_w,
              lru_conv_b, lru_w_a, lru_b_a, lru_w_i, lru_b_i, lru_lambda, lru_w_out,
              ffn_w_in, ffn_conv_w, ffn_conv_b, ffn_w_out):
    B = x.shape[0]
    meta = jnp.broadcast_to(meta_tokens.astype(x.dtype)[None], (B, N_META, D_MODEL))
    h = jnp.concatenate([meta, x], axis=1)
    pos = jnp.arange(h.shape[1], dtype=jnp.float32)
    for i in range(DEPTH):
        g = norm_gains[i]
        j = i // N_MIXERS
        a = rmsnorm(h, g[0])
        if i % N_MIXERS == 0:
            m = retention_mixer(a, ret_w_in[j], ret_w_out[j], pos)
        else:
            m = rglru_mixer(a, lru_w_in[j], lru_conv_w[j], lru_conv_b[j], lru_w_a[j],
                            lru_b_a[j], lru_w_i[j], lru_b_i[j], lru_lambda[j], lru_w_out[j])
        h = h + rmsnorm(m, g[1])
        f = conv_ffn(rmsnorm(h, g[2]), ffn_w_in[i], ffn_conv_w[i], ffn_conv_b[i], ffn_w_out[i])
        h = h + rmsnorm(f, g[3])
    return h[:, N_META:]
```

```python
import functools

import jax
import jax.numpy as jnp
from jax import lax
from jax.experimental import pallas as pl
from jax.experimental.pallas import tpu as pltpu

F32 = jnp.float32
BF16 = jnp.bfloat16

D_MODEL = 1024
N_META = 16
RMS_EPS = 1e-6

RET_HEADS = 4
RET_QK_DIM = D_MODEL // RET_HEADS
RET_V_DIM = 2 * D_MODEL // RET_HEADS
RET_CHUNK = 128
ROPE_BASE = 10000.0

LRU_BLOCKS = 4
LRU_BLOCK_W = D_MODEL // LRU_BLOCKS
LRU_C = 8.0
LRU_CONV_W = 4
LRU_CONV_LEFT = 2

FFN_HIDDEN = 3 * D_MODEL
FFN_CONV_W = 3
FFN_CONV_LEFT = 1

SUBLANES = 8
LANES = 128
BF16_ROWS = 16

VMEM_LIMIT_BYTES = 56 * 1024 * 1024


def _row_tile(t):
    for n in (2, 3, 4, 5, 6, 8):
        if t % n == 0 and (t // n) % BF16_ROWS == 0:
            return t // n
    return t


def _rms(x, g):
    ms = jnp.mean(x * x, axis=-1, keepdims=True)
    return x * lax.rsqrt(ms + RMS_EPS) * g


def _gelu_tanh(x):
    c = 0.7978845608028654
    return 0.5 * x * (1.0 + jnp.tanh(c * (x + 0.044715 * (x * x * x))))


def _params(sem):
    return pltpu.CompilerParams(dimension_semantics=sem, vmem_limit_bytes=VMEM_LIMIT_BYTES)


RET_IN_CHUNK = 512


def _ret_in_kernel(h_ref, g_ref, w_ref, cos_ref, sin_ref, p_ref, a_scr, *, rt, nrt):
    j = pl.program_id(1)

    @pl.when(j == 0)
    def _():
        for r in range(nrt):
            rows = pl.ds(r * rt, rt)
            a_scr[rows, :] = _rms(h_ref[0, rows, :], g_ref[...]).astype(BF16)

    def proj(r):
        return jnp.dot(a_scr[pl.ds(r * rt, rt), :], w_ref[...], preferred_element_type=F32)

    n_qk_steps = 2 * RET_HEADS * RET_QK_DIM // RET_IN_CHUNK
    half = RET_QK_DIM // 2

    @pl.when(j < n_qk_steps)
    def _():
        scale = jnp.where(j >= n_qk_steps // 2, RET_QK_DIM ** -0.5, 1.0).astype(F32)
        for r in range(nrt):
            rows = pl.ds(r * rt, rt)
            p = proj(r)
            c = cos_ref[rows, :]
            s = sin_ref[rows, :]
            for hh in range(RET_IN_CHUNK // RET_QK_DIM):
                lo = hh * RET_QK_DIM
                x1 = p[:, lo:lo + half]
                x2 = p[:, lo + half:lo + RET_QK_DIM]
                p_ref[0, rows, lo:lo + half] = ((x1 * c - x2 * s) * scale).astype(BF16)
                p_ref[0, rows, lo + half:lo + RET_QK_DIM] = ((x2 * c + x1 * s) * scale).astype(BF16)

    @pl.when(j >= n_qk_steps)
    def _():
        for r in range(nrt):
            p_ref[0, pl.ds(r * rt, rt), :] = proj(r).astype(BF16)


def _ret_in(h, gain, w_in, cos, sin):
    b, t, d = h.shape
    n_out = w_in.shape[1]
    rt = _row_tile(t)
    nrt = t // rt
    half = RET_QK_DIM // 2
    return pl.pallas_call(
        functools.partial(_ret_in_kernel, rt=rt, nrt=nrt),
        grid=(b, n_out // RET_IN_CHUNK),
        in_specs=[
            pl.BlockSpec((1, t, d), lambda i, j: (i, 0, 0)),
            pl.BlockSpec((1, d), lambda i, j: (0, 0)),
            pl.BlockSpec((d, RET_IN_CHUNK), lambda i, j: (0, j)),
            pl.BlockSpec((t, half), lambda i, j: (0, 0)),
            pl.BlockSpec((t, half), lambda i, j: (0, 0)),
        ],
        out_specs=pl.BlockSpec((1, t, RET_IN_CHUNK), lambda i, j: (i, 0, j)),
        out_shape=jax.ShapeDtypeStruct((b, t, n_out), BF16),
        scratch_shapes=[pltpu.VMEM((t, d), BF16)],
        compiler_params=_params(("arbitrary", "arbitrary")),
        name="ret_in_proj",
    )(h, gain.reshape(1, d), w_in, cos, sin)


def _retention_kernel(gc_ref, q_ref, k_ref, v_ref, g_ref, dm_ref, qf_ref, qb_ref, kf_ref, kb_ref,
                      y_ref, rf_scr, rb_scr, ob_scr, *, n_full, lead):
    c = RET_CHUNK
    gc = gc_ref[pl.program_id(1)]
    contract_rows = (((0,), (0,)), ((), ()))
    contract_feat = (((1,), (1,)), ((), ()))

    rf_scr[...] = jnp.zeros_like(rf_scr)
    rb_scr[...] = jnp.zeros_like(rb_scr)

    def bwd_chunk(rows, trows):
        qn = q_ref[0, rows, :]
        kn = k_ref[0, rows, :]
        vn = v_ref[0, rows, :]
        rb = rb_scr[...]
        ob_scr[rows, :] = qb_ref[0, trows, :] * jnp.dot(qn, rb.astype(BF16), preferred_element_type=F32)
        kd = (kn.astype(F32) * kb_ref[0, trows, :]).astype(BF16)
        rb_scr[...] = gc * rb + lax.dot_general(kd, vn, contract_rows, preferred_element_type=F32)

    def fwd_chunk(rows, trows):
        qn = q_ref[0, rows, :]
        kn = k_ref[0, rows, :]
        vn = v_ref[0, rows, :]
        rf = rf_scr[...]
        s = lax.dot_general(qn, kn, contract_feat, preferred_element_type=F32)
        p = (s * dm_ref[0, trows, trows]).astype(BF16)
        o = jnp.dot(p, vn, preferred_element_type=F32)
        o = o + qf_ref[0, trows, :] * jnp.dot(qn, rf.astype(BF16), preferred_element_type=F32)
        o = o + ob_scr[rows, :]
        kd = (kn.astype(F32) * kf_ref[0, trows, :]).astype(BF16)
        rf_scr[...] = gc * rf + lax.dot_general(kd, vn, contract_rows, preferred_element_type=F32)
        mu = jnp.mean(o, axis=-1, keepdims=True)
        oc = o - mu
        var = jnp.mean(oc * oc, axis=-1, keepdims=True)
        on = oc * lax.rsqrt(var + RMS_EPS)
        gg = g_ref[0, rows, :].astype(F32)
        y_ref[0, rows, :] = (gg * jax.nn.sigmoid(gg) * on).astype(BF16)

    def full_rows(n):
        return pl.ds(pl.multiple_of(lead + n * c, BF16_ROWS), c)

    all_t = pl.ds(0, c)
    lead_rows = pl.ds(0, lead)
    lead_t = pl.ds(c - lead, lead)

    def bwd_body(i, carry):
        bwd_chunk(full_rows(n_full - 1 - i), all_t)
        return carry

    lax.fori_loop(0, n_full, bwd_body, 0)
    bwd_chunk(lead_rows, lead_t)

    fwd_chunk(lead_rows, lead_t)

    def fwd_body(n, carry):
        fwd_chunk(full_rows(n), all_t)
        return carry

    lax.fori_loop(0, n_full, fwd_body, 0)


def _retention_tables():
    c = RET_CHUNK
    log_gamma = jnp.log1p(-jnp.exp2(-5.0 - jnp.arange(RET_HEADS, dtype=F32)))
    idx = jnp.arange(c, dtype=F32)
    lg = log_gamma[:, None, None]
    dmat = jnp.exp(lg * jnp.abs(idx[:, None] - idx[None, :])[None])
    qf = jnp.exp(log_gamma[:, None] * (idx + 1.0)[None])[..., None]
    qb = jnp.exp(log_gamma[:, None] * (c - idx)[None])[..., None]
    kf = jnp.exp(log_gamma[:, None] * (c - 1.0 - idx)[None])[..., None]
    kb = jnp.exp(log_gamma[:, None] * idx[None])[..., None]
    gc = jnp.exp(log_gamma * c)
    bq = lambda x: jnp.broadcast_to(x, (RET_HEADS, c, RET_V_DIM))
    bk = lambda x: jnp.broadcast_to(x, (RET_HEADS, c, RET_QK_DIM))
    return gc, dmat, bq(qf), bq(qb), bk(kf), bk(kb)


def _retention(p):
    b, t, _ = p.shape
    c = RET_CHUNK
    lead = t % c
    n_full = t // c
    assert lead % BF16_ROWS == 0 and lead > 0
    gc, dmat, qf, qb, kf, kb = _retention_tables()
    dk, dv, nh = RET_QK_DIM, RET_V_DIM, RET_HEADS
    k_off = nh * dk // dk
    v_off = 2 * nh * dk // dv
    g_off = v_off + nh
    tab = lambda w: pl.BlockSpec((1, c, w), lambda i, h: (h, 0, 0))
    return pl.pallas_call(
        functools.partial(_retention_kernel, n_full=n_full, lead=lead),
        grid=(b, nh),
        in_specs=[
            pl.BlockSpec(memory_space=pltpu.SMEM),
            pl.BlockSpec((1, t, dk), lambda i, h: (i, 0, h)),
            pl.BlockSpec((1, t, dk), lambda i, h: (i, 0, k_off + h)),
            pl.BlockSpec((1, t, dv), lambda i, h: (i, 0, v_off + h)),
            pl.BlockSpec((1, t, dv), lambda i, h: (i, 0, g_off + h)),
            tab(c), tab(dv), tab(dv), tab(dk), tab(dk),
        ],
        out_specs=pl.BlockSpec((1, t, dv), lambda i, h: (i, 0, h)),
        out_shape=jax.ShapeDtypeStruct((b, t, nh * dv), BF16),
        scratch_shapes=[
            pltpu.VMEM((dk, dv), F32),
            pltpu.VMEM((dk, dv), F32),
            pltpu.VMEM((t, dv), F32),
        ],
        compiler_params=_params(("arbitrary", "arbitrary")),
        name="retention",
    )(gc, p, p, p, p, dmat, qf, qb, kf, kb)


def _out_proj_kernel(y_ref, w_ref, h_ref, g_ref, o_ref):
    m = jnp.dot(y_ref[0], w_ref[...], preferred_element_type=F32)
    o_ref[0] = h_ref[0] + _rms(m, g_ref[...])


def _out_proj(y, w_out, h, gain):
    b, t, d = h.shape
    kdim = y.shape[-1]
    rt = _row_tile(t)
    return pl.pallas_call(
        _out_proj_kernel,
        grid=(b, t // rt),
        in_specs=[
            pl.BlockSpec((1, rt, kdim), lambda i, r: (i, r, 0)),
            pl.BlockSpec((kdim, d), lambda i, r: (0, 0)),
            pl.BlockSpec((1, rt, d), lambda i, r: (i, r, 0)),
            pl.BlockSpec((1, d), lambda i, r: (0, 0)),
        ],
        out_specs=pl.BlockSpec((1, rt, d), lambda i, r: (i, r, 0)),
        out_shape=jax.ShapeDtypeStruct((b, t, d), F32),
        compiler_params=_params(("arbitrary", "arbitrary")),
        name="out_proj",
    )(y, w_out, h, gain.reshape(1, d))


FFN_CHUNK = 1024
FFN_HALO = BF16_ROWS


def _ffn_kernel(hp_ref, h_ref, hn_ref, g_ref, wv_ref, wg_ref, cwv_ref, cwg_ref, cbv_ref, cbg_ref,
                wo_ref, o_ref, a_scr, uv_scr, ug_scr, *, rt, nrt, nj):
    r = pl.program_id(1)
    j = pl.program_id(2)
    halo = FFN_HALO

    @pl.when(j == 0)
    def _():
        g_pre = g_ref[0:1, :]
        a_scr[halo:halo + rt, :] = _rms(h_ref[0], g_pre).astype(BF16)
        prev = jnp.where(r > 0, _rms(hp_ref[0], g_pre), 0.0)
        a_scr[0:halo, :] = prev.astype(BF16)
        nxt = jnp.where(r < nrt - 1, _rms(hn_ref[0], g_pre), 0.0)
        a_scr[halo + rt:halo + rt + halo, :] = nxt.astype(BF16)
        o_ref[...] = jnp.zeros_like(o_ref)

    a = a_scr[...]
    uv_scr[...] = jnp.dot(a, wv_ref[...], preferred_element_type=F32)
    ug_scr[...] = jnp.dot(a, wg_ref[...], preferred_element_type=F32)

    def conv(u_scr, cw_ref, cb_ref):
        acc = cb_ref[...]
        for k in range(FFN_CONV_W):
            lo = halo - FFN_CONV_LEFT + k
            acc = acc + cw_ref[k:k + 1, :] * u_scr[lo:lo + rt, :]
        return acc

    val = conv(uv_scr, cwv_ref, cbv_ref)
    gate = conv(ug_scr, cwg_ref, cbg_ref)
    act = (_gelu_tanh(gate) * val).astype(BF16)
    o_ref[0] += jnp.dot(act, wo_ref[...], preferred_element_type=F32)

    @pl.when(j == nj - 1)
    def _():
        o_ref[0] = h_ref[0] + _rms(o_ref[0], g_ref[1:2, :])


def _ffn(h, gains, w_in, conv_w, conv_b, w_out):
    b, t, d = h.shape
    f = w_out.shape[0]
    rt = _row_tile(t)
    nrt = t // rt
    fc = FFN_CHUNK
    nj = f // fc
    halo = FFN_HALO
    hb = rt // halo
    last = t // halo - 1
    return pl.pallas_call(
        functools.partial(_ffn_kernel, rt=rt, nrt=nrt, nj=nj),
        grid=(b, nrt, nj),
        in_specs=[
            pl.BlockSpec((1, halo, d), lambda i, r, j: (i, jnp.maximum(r * hb - 1, 0), 0)),
            pl.BlockSpec((1, rt, d), lambda i, r, j: (i, r, 0)),
            pl.BlockSpec((1, halo, d), lambda i, r, j: (i, jnp.minimum((r + 1) * hb, last), 0)),
            pl.BlockSpec((2, d), lambda i, r, j: (0, 0)),
            pl.BlockSpec((d, fc), lambda i, r, j: (0, j)),
            pl.BlockSpec((d, fc), lambda i, r, j: (0, nj + j)),
            pl.BlockSpec((FFN_CONV_W, fc), lambda i, r, j: (0, j)),
            pl.BlockSpec((FFN_CONV_W, fc), lambda i, r, j: (0, nj + j)),
            pl.BlockSpec((1, fc), lambda i, r, j: (0, j)),
            pl.BlockSpec((1, fc), lambda i, r, j: (0, nj + j)),
            pl.BlockSpec((fc, d), lambda i, r, j: (j, 0)),
        ],
        out_specs=pl.BlockSpec((1, rt, d), lambda i, r, j: (i, r, 0)),
        out_shape=jax.ShapeDtypeStruct((b, t, d), F32),
        scratch_shapes=[
            pltpu.VMEM((rt + 2 * halo, d), BF16),
            pltpu.VMEM((rt + 2 * halo, fc), F32),
            pltpu.VMEM((rt + 2 * halo, fc), F32),
        ],
        compiler_params=_params(("arbitrary", "arbitrary", "arbitrary")),
        name="conv_ffn",
    )(h, h, h, gains, w_in, w_in, conv_w, conv_w, conv_b.reshape(1, 2 * f), conv_b.reshape(1, 2 * f), w_out)


LRU_PAD = SUBLANES


def _lru_kernel(h_ref, g_ref, wgate_ref, wx_ref, cw_ref, cb_ref, wa_ref, ba_ref, wi_ref, bi_ref,
                lam_ref, y_ref, a_scr, xp_scr, xc_scr, gate_scr, av_scr, u_scr, p_scr, ys_scr,
                *, rt, nrt, t):
    nb = pl.program_id(1)
    w = LRU_BLOCK_W
    nslab = w // LANES
    seg = t // SUBLANES
    pad = LRU_PAD

    @pl.when(nb == 0)
    def _():
        for r in range(nrt):
            rows = pl.ds(r * rt, rt)
            a_scr[rows, :] = _rms(h_ref[0, rows, :], g_ref[...]).astype(BF16)

    xp_scr[0:pad, :] = jnp.zeros((pad, w), F32)
    xp_scr[pad + t:pad + t + pad, :] = jnp.zeros((pad, w), F32)
    for r in range(nrt):
        a = a_scr[pl.ds(r * rt, rt), :]
        xp_scr[pl.ds(pad + r * rt, rt), :] = jnp.dot(a, wx_ref[...], preferred_element_type=F32)
        gate_scr[pl.ds(r * rt, rt), :] = jnp.dot(a, wgate_ref[...], preferred_element_type=F32)

    for r in range(nrt):
        acc = cb_ref[...]
        for k in range(LRU_CONV_W):
            lo = pad + r * rt - LRU_CONV_LEFT + k
            acc = acc + cw_ref[k:k + 1, :] * xp_scr[lo:lo + rt, :]
        xc_scr[pl.ds(r * rt, rt), :] = acc

    sub = lax.broadcasted_iota(jnp.int32, (SUBLANES, LANES), 0)

    def strided(i):
        return pl.ds(i, SUBLANES, stride=seg)

    for d in range(2):
        lam = lam_ref[d:d + 1, :]
        softplus_neg = jnp.maximum(-lam, 0.0) + jnp.log1p(jnp.exp(-jnp.abs(lam)))
        for r in range(nrt):
            rows = pl.ds(r * rt, rt)
            x = xc_scr[rows, :]
            xb = x.astype(BF16)
            ga = jnp.dot(xb, wa_ref[d, 0], preferred_element_type=F32) + ba_ref[d:d + 1, :]
            gi = jnp.dot(xb, wi_ref[d, 0], preferred_element_type=F32) + bi_ref[d:d + 1, :]
            log_a = -LRU_C * jax.nn.sigmoid(ga) * softplus_neg
            av = jnp.exp(log_a)
            mult = jnp.sqrt(-jnp.tanh(log_a) * (1.0 + av * av))
            uu = mult * (jax.nn.sigmoid(gi) * x)
            for s in range(nslab):
                av_scr[s, rows, :] = av[:, s * LANES:(s + 1) * LANES]
                u_scr[s, rows, :] = uu[:, s * LANES:(s + 1) * LANES]

        def scan_body(ii, carry):
            i = ii if d == 0 else seg - 1 - ii
            out = []
            for s in range(nslab):
                hc, pc = carry[s]
                a_i = av_scr[s, strided(i), :]
                hc = a_i * hc + u_scr[s, strided(i), :]
                pc = a_i * pc
                u_scr[s, strided(i), :] = hc
                p_scr[s, strided(i), :] = pc
                out.append((hc, pc))
            return tuple(out)

        init = tuple((jnp.zeros((SUBLANES, LANES), F32), jnp.ones((SUBLANES, LANES), F32))
                     for _ in range(nslab))
        ends = lax.fori_loop(0, seg, scan_body, init)

        cins = []
        for s in range(nslab):
            h_end, p_end = ends[s]
            shift = 1 if d == 0 else SUBLANES - 1
            edge = 0 if d == 0 else SUBLANES - 1
            total = h_end
            for _ in range(SUBLANES - 1):
                moved = jnp.where(sub == edge, 0.0, pltpu.roll(total, shift, 0))
                total = h_end + p_end * moved
            cins.append(jnp.where(sub == edge, 0.0, pltpu.roll(total, shift, 0)))

        def fix_body(i, carry):
            for s in range(nslab):
                hfin = u_scr[s, strided(i), :] + p_scr[s, strided(i), :] * cins[s]
                if d == 0:
                    ys_scr[s, strided(i), :] = hfin
                else:
                    ys_scr[s, strided(i), :] = ys_scr[s, strided(i), :] + hfin
            return carry

        lax.fori_loop(0, seg, fix_body, 0)

    for r in range(nrt):
        rows = pl.ds(r * rt, rt)
        gl = _gelu_tanh(gate_scr[rows, :])
        for s in range(nslab):
            y_ref[0, rows, s * LANES:(s + 1) * LANES] = (
                ys_scr[s, rows, :] * gl[:, s * LANES:(s + 1) * LANES]).astype(BF16)


def _lru(h, gain, w_in, conv_w, conv_b, w_a, b_a, w_i, b_i, lam):
    b, t, d = h.shape
    r_width = w_in.shape[1] // 2
    w = LRU_BLOCK_W
    nblk = r_width // w
    rt = _row_tile(t)
    nrt = t // rt
    assert t % SUBLANES == 0
    nslab = w // LANES
    vec = lambda rows: pl.BlockSpec((rows, w), lambda i, n: (0, n))
    return pl.pallas_call(
        functools.partial(_lru_kernel, rt=rt, nrt=nrt, t=t),
        grid=(b, nblk),
        in_specs=[
            pl.BlockSpec((1, t, d), lambda i, n: (i, 0, 0)),
            pl.BlockSpec((1, d), lambda i, n: (0, 0)),
            pl.BlockSpec((d, w), lambda i, n: (0, n)),
            pl.BlockSpec((d, w), lambda i, n: (0, nblk + n)),
            vec(LRU_CONV_W), vec(1),
            pl.BlockSpec((2, 1, w, w), lambda i, n: (0, n, 0, 0)), vec(2),
            pl.BlockSpec((2, 1, w, w), lambda i, n: (0, n, 0, 0)), vec(2),
            vec(2),
        ],
        out_specs=pl.BlockSpec((1, t, w), lambda i, n: (i, 0, n)),
        out_shape=jax.ShapeDtypeStruct((b, t, r_width), BF16),
        scratch_shapes=[
            pltpu.VMEM((t, d), BF16),
            pltpu.VMEM((t + 2 * LRU_PAD, w), F32),
            pltpu.VMEM((t, w), F32),
            pltpu.VMEM((t, w), F32),
            pltpu.VMEM((nslab, t, LANES), F32),
            pltpu.VMEM((nslab, t, LANES), F32),
            pltpu.VMEM((nslab, t, LANES), F32),
            pltpu.VMEM((nslab, t, LANES), F32),
        ],
        compiler_params=_params(("arbitrary", "arbitrary")),
        name="rglru",
    )(h, gain.reshape(1, d), w_in, w_in, conv_w, conv_b.reshape(1, r_width), w_a, b_a, w_i, b_i, lam)


def kernel(x, meta_tokens, norm_gains, ret_w_in, ret_w_out, lru_w_in, lru_conv_w, lru_conv_b,
           lru_w_a, lru_b_a, lru_w_i, lru_b_i, lru_lambda, lru_w_out, ffn_w_in, ffn_conv_w,
           ffn_conv_b, ffn_w_out):
    b = x.shape[0]
    meta = jnp.broadcast_to(meta_tokens.astype(x.dtype)[None], (b, N_META, D_MODEL))
    h = jnp.concatenate([meta, x], axis=1)
    t = h.shape[1]

    pos = jnp.arange(t, dtype=F32)
    inv = ROPE_BASE ** (-jnp.arange(0, RET_QK_DIM, 2, dtype=F32) / RET_QK_DIM)
    ang = pos[:, None] * inv[None, :]
    cos, sin = jnp.cos(ang), jnp.sin(ang)

    g = norm_gains[0]
    p = _ret_in(h, g[0], ret_w_in[0].astype(BF16), cos, sin)
    y = _retention(p)
    h = _out_proj(y, ret_w_out[0].astype(BF16), h, g[1])
    h = _ffn(h, g[2:4], ffn_w_in[0].astype(BF16), ffn_conv_w[0], ffn_conv_b[0], ffn_w_out[0].astype(BF16))

    g = norm_gains[1]
    y = _lru(h, g[0], lru_w_in[0].astype(BF16), lru_conv_w[0], lru_conv_b[0],
             lru_w_a[0].astype(BF16), lru_b_a[0], lru_w_i[0].astype(BF16), lru_b_i[0], lru_lambda[0])
    h = _out_proj(y, lru_w_out[0].astype(BF16), h, g[1])
    h = _ffn(h, g[2:4], ffn_w_in[1].astype(BF16), ffn_conv_w[1], ffn_conv_b[1], ffn_w_out[1].astype(BF16))
    return h[:, N_META:]
```

```python
import functools

import jax
import jax.numpy as jnp
from jax import lax
from jax.experimental import pallas as pl
from jax.experimental.pallas import tpu as pltpu

F32 = jnp.float32
BF16 = jnp.bfloat16

D_MODEL = 1024
N_META = 16
RMS_EPS = 1e-6

RET_HEADS = 4
RET_QK_DIM = D_MODEL // RET_HEADS
RET_V_DIM = 2 * D_MODEL // RET_HEADS
RET_CHUNK = 256
ROPE_BASE = 10000.0

LRU_BLOCKS = 4
LRU_BLOCK_W = D_MODEL // LRU_BLOCKS
LRU_C = 8.0
LRU_CONV_W = 4
LRU_CONV_LEFT = 2

FFN_HIDDEN = 3 * D_MODEL
FFN_CONV_W = 3
FFN_CONV_LEFT = 1

SUBLANES = 8
LANES = 128
BF16_ROWS = 16

VMEM_LIMIT_BYTES = 56 * 1024 * 1024


def _row_tile(t):
    for n in (2, 3, 4, 5, 6, 8):
        if t % n == 0 and (t // n) % BF16_ROWS == 0:
            return t // n
    return t


def _rms(x, g):
    ms = jnp.mean(x * x, axis=-1, keepdims=True)
    return x * lax.rsqrt(ms + RMS_EPS) * g


def _gelu_tanh(x):
    c = 0.7978845608028654
    return 0.5 * x * (1.0 + jnp.tanh(c * (x + 0.044715 * (x * x * x))))


def _params(sem):
    return pltpu.CompilerParams(dimension_semantics=sem, vmem_limit_bytes=VMEM_LIMIT_BYTES)


RET_IN_CHUNK = 1024


def _ret_in_kernel(h_ref, g_ref, w_ref, cos_ref, sin_ref, p_ref, a_scr, *, rt, nrt):
    j = pl.program_id(1)

    @pl.when(j == 0)
    def _():
        for r in range(nrt):
            rows = pl.ds(r * rt, rt)
            a_scr[rows, :] = _rms(h_ref[0, rows, :], g_ref[...]).astype(BF16)

    def proj(r):
        return jnp.dot(a_scr[pl.ds(r * rt, rt), :], w_ref[...], preferred_element_type=F32)

    n_qk_steps = 2 * RET_HEADS * RET_QK_DIM // RET_IN_CHUNK
    half = RET_QK_DIM // 2

    @pl.when(j < n_qk_steps)
    def _():
        scale = jnp.where(j >= n_qk_steps // 2, RET_QK_DIM ** -0.5, 1.0).astype(F32)
        for r in range(nrt):
            rows = pl.ds(r * rt, rt)
            p = proj(r)
            c = cos_ref[rows, :]
            s = sin_ref[rows, :]
            for hh in range(RET_IN_CHUNK // RET_QK_DIM):
                lo = hh * RET_QK_DIM
                x1 = p[:, lo:lo + half]
                x2 = p[:, lo + half:lo + RET_QK_DIM]
                p_ref[0, rows, lo:lo + half] = ((x1 * c - x2 * s) * scale).astype(BF16)
                p_ref[0, rows, lo + half:lo + RET_QK_DIM] = ((x2 * c + x1 * s) * scale).astype(BF16)

    @pl.when(j >= n_qk_steps)
    def _():
        for r in range(nrt):
            p_ref[0, pl.ds(r * rt, rt), :] = proj(r).astype(BF16)


def _ret_in(h, gain, w_in, cos, sin):
    b, t, d = h.shape
    n_out = w_in.shape[1]
    rt = _row_tile(t)
    nrt = t // rt
    half = RET_QK_DIM // 2
    return pl.pallas_call(
        functools.partial(_ret_in_kernel, rt=rt, nrt=nrt),
        grid=(b, n_out // RET_IN_CHUNK),
        in_specs=[
            pl.BlockSpec((1, t, d), lambda i, j: (i, 0, 0)),
            pl.BlockSpec((1, d), lambda i, j: (0, 0)),
            pl.BlockSpec((d, RET_IN_CHUNK), lambda i, j: (0, j)),
            pl.BlockSpec((t, half), lambda i, j: (0, 0)),
            pl.BlockSpec((t, half), lambda i, j: (0, 0)),
        ],
        out_specs=pl.BlockSpec((1, t, RET_IN_CHUNK), lambda i, j: (i, 0, j)),
        out_shape=jax.ShapeDtypeStruct((b, t, n_out), BF16),
        scratch_shapes=[pltpu.VMEM((t, d), BF16)],
        compiler_params=_params(("arbitrary", "arbitrary")),
        name="ret_in_proj",
    )(h, gain.reshape(1, d), w_in, cos, sin)


def _retention_kernel(gc_ref, q_ref, k_ref, v_ref, g_ref, dm_ref, qf_ref, qb_ref, kf_ref, kb_ref,
                      y_ref, rf_scr, rb_scr, ob_scr, *, n_full, lead):
    c = RET_CHUNK
    gc = gc_ref[pl.program_id(1)]
    contract_rows = (((0,), (0,)), ((), ()))
    contract_feat = (((1,), (1,)), ((), ()))

    def state_update(r_scr, kn, vn, decay, first):
        kd = (kn.astype(F32) * decay).astype(BF16)
        kv = lax.dot_general(kd, vn, contract_rows, preferred_element_type=F32)
        r_scr[...] = kv if first else gc * r_scr[...] + kv

    def bwd_chunk(rows, trows, first, update):
        qn = q_ref[0, rows, :]
        if first:
            ob_scr[rows, :] = jnp.zeros((rows.size, RET_V_DIM), F32)
        else:
            inter = jnp.dot(qn, rb_scr[...].astype(BF16), preferred_element_type=F32)
            ob_scr[rows, :] = qb_ref[0, trows, :] * inter
        if update:
            state_update(rb_scr, k_ref[0, rows, :], v_ref[0, rows, :], kb_ref[0, trows, :], first)

    def fwd_chunk(rows, trows, first, update):
        qn = q_ref[0, rows, :]
        kn = k_ref[0, rows, :]
        vn = v_ref[0, rows, :]
        s = lax.dot_general(qn, kn, contract_feat, preferred_element_type=F32)
        p = (s * dm_ref[0, trows, trows]).astype(BF16)
        o = jnp.dot(p, vn, preferred_element_type=F32) + ob_scr[rows, :]
        if not first:
            inter = jnp.dot(qn, rf_scr[...].astype(BF16), preferred_element_type=F32)
            o = o + qf_ref[0, trows, :] * inter
        if update:
            state_update(rf_scr, kn, vn, kf_ref[0, trows, :], first)
        mu = jnp.mean(o, axis=-1, keepdims=True)
        oc = o - mu
        var = jnp.mean(oc * oc, axis=-1, keepdims=True)
        on = oc * lax.rsqrt(var + RMS_EPS)
        gg = g_ref[0, rows, :].astype(F32)
        y_ref[0, rows, :] = (gg * jax.nn.sigmoid(gg) * on).astype(BF16)

    all_t = pl.ds(0, c)
    lead_rows = pl.ds(0, lead)
    lead_t = pl.ds(c - lead, lead)
    full_rows = lambda n: pl.ds(lead + n * c, c)

    for n in reversed(range(n_full)):
        bwd_chunk(full_rows(n), all_t, first=(n == n_full - 1), update=True)
    bwd_chunk(lead_rows, lead_t, first=False, update=False)

    fwd_chunk(lead_rows, lead_t, first=True, update=True)
    for n in range(n_full):
        fwd_chunk(full_rows(n), all_t, first=False, update=(n < n_full - 1))


def _retention_tables():
    c = RET_CHUNK
    log_gamma = jnp.log1p(-jnp.exp2(-5.0 - jnp.arange(RET_HEADS, dtype=F32)))
    idx = jnp.arange(c, dtype=F32)
    lg = log_gamma[:, None, None]
    dmat = jnp.exp(lg * jnp.abs(idx[:, None] - idx[None, :])[None])
    qf = jnp.exp(log_gamma[:, None] * (idx + 1.0)[None])[..., None]
    qb = jnp.exp(log_gamma[:, None] * (c - idx)[None])[..., None]
    kf = jnp.exp(log_gamma[:, None] * (c - 1.0 - idx)[None])[..., None]
    kb = jnp.exp(log_gamma[:, None] * idx[None])[..., None]
    gc = jnp.exp(log_gamma * c)
    bq = lambda x: jnp.broadcast_to(x, (RET_HEADS, c, RET_V_DIM))
    bk = lambda x: jnp.broadcast_to(x, (RET_HEADS, c, RET_QK_DIM))
    return gc, dmat, bq(qf), bq(qb), bk(kf), bk(kb)


def _retention(p):
    b, t, _ = p.shape
    c = RET_CHUNK
    lead = t % c
    n_full = t // c
    assert lead % BF16_ROWS == 0 and lead > 0
    gc, dmat, qf, qb, kf, kb = _retention_tables()
    dk, dv, nh = RET_QK_DIM, RET_V_DIM, RET_HEADS
    k_off = nh * dk // dk
    v_off = 2 * nh * dk // dv
    g_off = v_off + nh
    tab = lambda w: pl.BlockSpec((1, c, w), lambda i, h: (h, 0, 0))
    return pl.pallas_call(
        functools.partial(_retention_kernel, n_full=n_full, lead=lead),
        grid=(b, nh),
        in_specs=[
            pl.BlockSpec(memory_space=pltpu.SMEM),
            pl.BlockSpec((1, t, dk), lambda i, h: (i, 0, h)),
            pl.BlockSpec((1, t, dk), lambda i, h: (i, 0, k_off + h)),
            pl.BlockSpec((1, t, dv), lambda i, h: (i, 0, v_off + h)),
            pl.BlockSpec((1, t, dv), lambda i, h: (i, 0, g_off + h)),
            tab(c), tab(dv), tab(dv), tab(dk), tab(dk),
        ],
        out_specs=pl.BlockSpec((1, t, dv), lambda i, h: (i, 0, h)),
        out_shape=jax.ShapeDtypeStruct((b, t, nh * dv), BF16),
        scratch_shapes=[
            pltpu.VMEM((dk, dv), F32),
            pltpu.VMEM((dk, dv), F32),
            pltpu.VMEM((t, dv), F32),
        ],
        compiler_params=_params(("arbitrary", "arbitrary")),
        name="retention",
    )(gc, p, p, p, p, dmat, qf, qb, kf, kb)


def _out_proj_kernel(y_ref, w_ref, h_ref, g_ref, o_ref):
    m = jnp.dot(y_ref[0], w_ref[...], preferred_element_type=F32)
    o_ref[0] = h_ref[0] + _rms(m, g_ref[...])


def _out_proj(y, w_out, h, gain):
    b, t, d = h.shape
    kdim = y.shape[-1]
    rt = _row_tile(t)
    return pl.pallas_call(
        _out_proj_kernel,
        grid=(b, t // rt),
        in_specs=[
            pl.BlockSpec((1, rt, kdim), lambda i, r: (i, r, 0)),
            pl.BlockSpec((kdim, d), lambda i, r: (0, 0)),
            pl.BlockSpec((1, rt, d), lambda i, r: (i, r, 0)),
            pl.BlockSpec((1, d), lambda i, r: (0, 0)),
        ],
        out_specs=pl.BlockSpec((1, rt, d), lambda i, r: (i, r, 0)),
        out_shape=jax.ShapeDtypeStruct((b, t, d), F32),
        compiler_params=_params(("arbitrary", "arbitrary")),
        name="out_proj",
    )(y, w_out, h, gain.reshape(1, d))


FFN_CHUNK = 1024
FFN_SUB = 256
FFN_HALO = BF16_ROWS


def _ffn_kernel(hp_ref, h_ref, hn_ref, g_ref, wv_ref, wg_ref, cwv_ref, cwg_ref, cbv_ref, cbg_ref,
                wo_ref, o_ref, a_scr, uv_scr, ug_scr, *, rt, nrt, nj, fc):
    r = pl.program_id(1)
    j = pl.program_id(2)
    halo = FFN_HALO

    @pl.when(j == 0)
    def _():
        g_pre = g_ref[0:1, :]
        a_scr[halo:halo + rt, :] = _rms(h_ref[0], g_pre).astype(BF16)
        prev = jnp.where(r > 0, _rms(hp_ref[0], g_pre), 0.0)
        a_scr[0:halo, :] = prev.astype(BF16)
        nxt = jnp.where(r < nrt - 1, _rms(hn_ref[0], g_pre), 0.0)
        a_scr[halo + rt:halo + rt + halo, :] = nxt.astype(BF16)
        o_ref[...] = jnp.zeros_like(o_ref)

    a = a_scr[...]
    spc = FFN_SUB // LANES
    nsub = fc // FFN_SUB

    def up_proj(c):
        cols = slice(c * FFN_SUB, (c + 1) * FFN_SUB)
        uv = jnp.dot(a, wv_ref[:, cols], preferred_element_type=F32)
        ug = jnp.dot(a, wg_ref[:, cols], preferred_element_type=F32)
        for s in range(spc):
            uv_scr[c * spc + s] = uv[:, s * LANES:(s + 1) * LANES]
            ug_scr[c * spc + s] = ug[:, s * LANES:(s + 1) * LANES]

    def conv(u_scr, c, cw_ref, cb_ref):
        outs = []
        for s in range(spc):
            lanes = slice(c * FFN_SUB + s * LANES, c * FFN_SUB + (s + 1) * LANES)
            acc = cb_ref[:, lanes]
            for k in range(FFN_CONV_W):
                lo = halo - FFN_CONV_LEFT + k
                acc = acc + cw_ref[k:k + 1, lanes] * u_scr[c * spc + s, lo:lo + rt, :]
            outs.append(acc)
        return jnp.concatenate(outs, axis=1)

    acc = None
    up_proj(0)
    for c in range(nsub):
        if c + 1 < nsub:
            up_proj(c + 1)
        val = conv(uv_scr, c, cwv_ref, cbv_ref)
        gate = conv(ug_scr, c, cwg_ref, cbg_ref)
        act = (_gelu_tanh(gate) * val).astype(BF16)
        part = jnp.dot(act, wo_ref[c * FFN_SUB:(c + 1) * FFN_SUB, :], preferred_element_type=F32)
        acc = part if acc is None else acc + part
    o_ref[0] += acc

    @pl.when(j == nj - 1)
    def _():
        o_ref[0] = h_ref[0] + _rms(o_ref[0], g_ref[1:2, :])


def _ffn(h, gains, w_in, conv_w, conv_b, w_out):
    b, t, d = h.shape
    f = w_out.shape[0]
    rt = _row_tile(t)
    nrt = t // rt
    fc = FFN_CHUNK
    nj = f // fc
    halo = FFN_HALO
    hb = rt // halo
    last = t // halo - 1
    return pl.pallas_call(
        functools.partial(_ffn_kernel, rt=rt, nrt=nrt, nj=nj, fc=fc),
        grid=(b, nrt, nj),
        in_specs=[
            pl.BlockSpec((1, halo, d), lambda i, r, j: (i, jnp.maximum(r * hb - 1, 0), 0)),
            pl.BlockSpec((1, rt, d), lambda i, r, j: (i, r, 0)),
            pl.BlockSpec((1, halo, d), lambda i, r, j: (i, jnp.minimum((r + 1) * hb, last), 0)),
            pl.BlockSpec((2, d), lambda i, r, j: (0, 0)),
            pl.BlockSpec((d, fc), lambda i, r, j: (0, j)),
            pl.BlockSpec((d, fc), lambda i, r, j: (0, nj + j)),
            pl.BlockSpec((FFN_CONV_W, fc), lambda i, r, j: (0, j)),
            pl.BlockSpec((FFN_CONV_W, fc), lambda i, r, j: (0, nj + j)),
            pl.BlockSpec((1, fc), lambda i, r, j: (0, j)),
            pl.BlockSpec((1, fc), lambda i, r, j: (0, nj + j)),
            pl.BlockSpec((fc, d), lambda i, r, j: (j, 0)),
        ],
        out_specs=pl.BlockSpec((1, rt, d), lambda i, r, j: (i, r, 0)),
        out_shape=jax.ShapeDtypeStruct((b, t, d), F32),
        scratch_shapes=[
            pltpu.VMEM((rt + 2 * halo, d), BF16),
            pltpu.VMEM((fc // LANES, rt + 2 * halo, LANES), F32),
            pltpu.VMEM((fc // LANES, rt + 2 * halo, LANES), F32),
        ],
        compiler_params=_params(("arbitrary", "arbitrary", "arbitrary")),
        name="conv_ffn",
    )(h, h, h, gains, w_in, w_in, conv_w, conv_w, conv_b.reshape(1, 2 * f), conv_b.reshape(1, 2 * f), w_out)


LRU_PAD = SUBLANES
LRU_SCAN_UNROLL = 6


def _lru_kernel(h_ref, g_ref, wgate_ref, wx_ref, cw_ref, cb_ref, wa_ref, ba_ref, wi_ref, bi_ref,
                lam_ref, y_ref, a_scr, xp_scr, xc_scr, gate_scr, av_scr, u_scr, hs_scr,
                *, rt, nrt, t):
    nb = pl.program_id(1)
    w = LRU_BLOCK_W
    nslab = w // LANES
    seg = t // SUBLANES
    pad = LRU_PAD
    lanes = lambda s: slice(s * LANES, (s + 1) * LANES)

    @pl.when(nb == 0)
    def _():
        for r in range(nrt):
            rows = pl.ds(r * rt, rt)
            a_scr[rows, :] = _rms(h_ref[0, rows, :], g_ref[...]).astype(BF16)

    for s in range(nslab):
        xp_scr[s, 0:pad, :] = jnp.zeros((pad, LANES), F32)
        xp_scr[s, pad + t:pad + t + pad, :] = jnp.zeros((pad, LANES), F32)

    def in_proj(r):
        a = a_scr[pl.ds(r * rt, rt), :]
        xr = jnp.dot(a, wx_ref[...], preferred_element_type=F32)
        for s in range(nslab):
            xp_scr[s, pl.ds(pad + r * rt, rt), :] = xr[:, lanes(s)]
        gate_scr[pl.ds(r * rt, rt), :] = jnp.dot(a, wgate_ref[...], preferred_element_type=F32)

    def conv(r):
        outs = []
        for s in range(nslab):
            acc = cb_ref[:, lanes(s)]
            for k in range(LRU_CONV_W):
                lo = pad + r * rt - LRU_CONV_LEFT + k
                acc = acc + cw_ref[k:k + 1, lanes(s)] * xp_scr[s, lo:lo + rt, :]
            outs.append(acc)
        xc_scr[pl.ds(r * rt, rt), :] = jnp.concatenate(outs, axis=1)

    def gate_dots(r):
        xb = xc_scr[pl.ds(r * rt, rt), :].astype(BF16)
        return [(jnp.dot(xb, wa_ref[d, 0], preferred_element_type=F32),
                 jnp.dot(xb, wi_ref[d, 0], preferred_element_type=F32)) for d in range(2)]

    def gate_elementwise(r, raw):
        rows = pl.ds(r * rt, rt)
        x = xc_scr[rows, :]
        hx = 0.5 * x
        for d in range(2):
            ga_raw, gi_raw = raw[d]
            lam = lam_ref[d:d + 1, :]
            softplus_neg = jnp.maximum(-lam, 0.0) + jnp.log1p(jnp.exp(-jnp.abs(lam)))
            c1 = (-0.5 * LRU_C) * softplus_neg
            t_a = jnp.tanh(0.5 * ga_raw + 0.5 * ba_ref[d:d + 1, :])
            log_a = c1 * t_a + c1
            av = jnp.exp(log_a)
            z = jnp.tanh(log_a) * (-1.0 - av * av)
            mult = jnp.where(z > 0.0, z * lax.rsqrt(z), 0.0)
            t_i = jnp.tanh(0.5 * gi_raw + 0.5 * bi_ref[d:d + 1, :])
            uu = mult * (hx * t_i + hx)
            for s in range(nslab):
                av_scr[d, s, rows, :] = av[:, lanes(s)]
                u_scr[d, s, rows, :] = uu[:, lanes(s)]

    in_proj(0)
    raw = None
    for r in range(nrt):
        if r + 1 < nrt:
            in_proj(r + 1)
        conv(r)
        if raw is not None:
            gate_elementwise(r - 1, raw)
        raw = gate_dots(r)
    gate_elementwise(nrt - 1, raw)

    sub = lax.broadcasted_iota(jnp.int32, (SUBLANES, LANES), 0)
    chains = [(d, s) for d in range(2) for s in range(nslab)]

    def strided(d, ii):
        i = ii if d == 0 else seg - 1 - ii
        return pl.ds(i, SUBLANES, stride=seg)

    def pass1(ii, carry):
        out = []
        for (d, s), (hc, pc) in zip(chains, carry):
            a_i = av_scr[d, s, strided(d, ii), :]
            out.append((a_i * hc + u_scr[d, s, strided(d, ii), :], a_i * pc))
        return tuple(out)

    init = tuple((jnp.zeros((SUBLANES, LANES), F32), jnp.ones((SUBLANES, LANES), F32))
                 for _ in chains)
    ends = lax.fori_loop(0, seg, pass1, init, unroll=LRU_SCAN_UNROLL)

    cins = []
    for (d, s), (h_end, p_end) in zip(chains, ends):
        shift = 1 if d == 0 else SUBLANES - 1
        edge = 0 if d == 0 else SUBLANES - 1
        total = h_end
        for _ in range(SUBLANES - 1):
            moved = jnp.where(sub == edge, 0.0, pltpu.roll(total, shift, 0))
            total = h_end + p_end * moved
        cins.append(jnp.where(sub == edge, 0.0, pltpu.roll(total, shift, 0)))

    def pass2(ii, carry):
        out = []
        for (d, s), hc in zip(chains, carry):
            hc = av_scr[d, s, strided(d, ii), :] * hc + u_scr[d, s, strided(d, ii), :]
            hs_scr[d, s, strided(d, ii), :] = hc
            out.append(hc)
        return tuple(out)

    lax.fori_loop(0, seg, pass2, tuple(cins), unroll=LRU_SCAN_UNROLL)

    for r in range(nrt):
        rows = pl.ds(r * rt, rt)
        gl = _gelu_tanh(gate_scr[rows, :])
        for s in range(nslab):
            y_ref[0, rows, lanes(s)] = (
                (hs_scr[0, s, rows, :] + hs_scr[1, s, rows, :]) * gl[:, lanes(s)]).astype(BF16)


def _lru(h, gain, w_in, conv_w, conv_b, w_a, b_a, w_i, b_i, lam):
    b, t, d = h.shape
    r_width = w_in.shape[1] // 2
    w = LRU_BLOCK_W
    nblk = r_width // w
    rt = _row_tile(t)
    nrt = t // rt
    assert t % SUBLANES == 0
    nslab = w // LANES
    vec = lambda rows: pl.BlockSpec((rows, w), lambda i, n: (0, n))
    return pl.pallas_call(
        functools.partial(_lru_kernel, rt=rt, nrt=nrt, t=t),
        grid=(b, nblk),
        in_specs=[
            pl.BlockSpec((1, t, d), lambda i, n: (i, 0, 0)),
            pl.BlockSpec((1, d), lambda i, n: (0, 0)),
            pl.BlockSpec((d, w), lambda i, n: (0, n)),
            pl.BlockSpec((d, w), lambda i, n: (0, nblk + n)),
            vec(LRU_CONV_W), vec(1),
            pl.BlockSpec((2, 1, w, w), lambda i, n: (0, n, 0, 0)), vec(2),
            pl.BlockSpec((2, 1, w, w), lambda i, n: (0, n, 0, 0)), vec(2),
            vec(2),
        ],
        out_specs=pl.BlockSpec((1, t, w), lambda i, n: (i, 0, n)),
        out_shape=jax.ShapeDtypeStruct((b, t, r_width), BF16),
        scratch_shapes=[
            pltpu.VMEM((t, d), BF16),
            pltpu.VMEM((nslab, t + 2 * LRU_PAD, LANES), F32),
            pltpu.VMEM((t, w), F32),
            pltpu.VMEM((t, w), F32),
            pltpu.VMEM((2, nslab, t, LANES), F32),
            pltpu.VMEM((2, nslab, t, LANES), F32),
            pltpu.VMEM((2, nslab, t, LANES), F32),
        ],
        compiler_params=_params(("arbitrary", "arbitrary")),
        name="rglru",
    )(h, gain.reshape(1, d), w_in, w_in, conv_w, conv_b.reshape(1, r_width), w_a, b_a, w_i, b_i, lam)


def kernel(x, meta_tokens, norm_gains, ret_w_in, ret_w_out, lru_w_in, lru_conv_w, lru_conv_b,
           lru_w_a, lru_b_a, lru_w_i, lru_b_i, lru_lambda, lru_w_out, ffn_w_in, ffn_conv_w,
           ffn_conv_b, ffn_w_out):
    b = x.shape[0]
    meta = jnp.broadcast_to(meta_tokens.astype(x.dtype)[None], (b, N_META, D_MODEL))
    h = jnp.concatenate([meta, x], axis=1)
    t = h.shape[1]

    pos = jnp.arange(t, dtype=F32)
    inv = ROPE_BASE ** (-jnp.arange(0, RET_QK_DIM, 2, dtype=F32) / RET_QK_DIM)
    ang = pos[:, None] * inv[None, :]
    cos, sin = jnp.cos(ang), jnp.sin(ang)

    g = norm_gains[0]
    p = _ret_in(h, g[0], ret_w_in[0].astype(BF16), cos, sin)
    y = _retention(p)
    h = _out_proj(y, ret_w_out[0].astype(BF16), h, g[1])
    h = _ffn(h, g[2:4], ffn_w_in[0].astype(BF16), ffn_conv_w[0], ffn_conv_b[0], ffn_w_out[0].astype(BF16))

    g = norm_gains[1]
    y = _lru(h, g[0], lru_w_in[0].astype(BF16), lru_conv_w[0], lru_conv_b[0],
             lru_w_a[0].astype(BF16), lru_b_a[0], lru_w_i[0].astype(BF16), lru_b_i[0], lru_lambda[0])
    h = _out_proj(y, lru_w_out[0].astype(BF16), h, g[1])
    h = _ffn(h, g[2:4], ffn_w_in[1].astype(BF16), ffn_conv_w[1], ffn_conv_b[1], ffn_w_out[1].astype(BF16))
    return h[:, N_META:]
```

```python
import functools

import jax
import jax.numpy as jnp
from jax import lax
from jax.experimental import pallas as pl
from jax.experimental.pallas import tpu as pltpu

F32 = jnp.float32
BF16 = jnp.bfloat16

D_MODEL = 1024
N_META = 16
RMS_EPS = 1e-6

RET_HEADS = 4
RET_QK_DIM = D_MODEL // RET_HEADS
RET_V_DIM = 2 * D_MODEL // RET_HEADS
RET_CHUNK = 256
ROPE_BASE = 10000.0

LRU_BLOCKS = 4
LRU_BLOCK_W = D_MODEL // LRU_BLOCKS
LRU_C = 8.0
LRU_CONV_W = 4
LRU_CONV_LEFT = 2

FFN_HIDDEN = 3 * D_MODEL
FFN_CONV_W = 3
FFN_CONV_LEFT = 1

SUBLANES = 8
LANES = 128
BF16_ROWS = 16

VMEM_LIMIT_BYTES = 56 * 1024 * 1024


def _row_tile(t):
    for n in (2, 3, 4, 5, 6, 8):
        if t % n == 0 and (t // n) % BF16_ROWS == 0:
            return t // n
    return t


def _rms(x, g):
    ms = jnp.mean(x * x, axis=-1, keepdims=True)
    return x * lax.rsqrt(ms + RMS_EPS) * g


def _gelu_tanh(x):
    c = 0.7978845608028654
    return 0.5 * x * (1.0 + jnp.tanh(c * (x + 0.044715 * (x * x * x))))


def _params(sem):
    return pltpu.CompilerParams(dimension_semantics=sem, vmem_limit_bytes=VMEM_LIMIT_BYTES)


RET_IN_CHUNK = 1024


def _ret_in_kernel(x_ref, meta_ref, g_ref, w_ref, cos_ref, sin_ref, p_ref, a_scr, *, rt, nrt):
    j = pl.program_id(1)
    n_meta = meta_ref.shape[0]
    seq = x_ref.shape[1]
    xt = _row_tile(seq)

    @pl.when(j == 0)
    def _():
        a_scr[0:n_meta, :] = _rms(meta_ref[...], g_ref[...]).astype(BF16)
        for r in range(seq // xt):
            a_scr[pl.ds(n_meta + r * xt, xt), :] = _rms(
                x_ref[0, pl.ds(r * xt, xt), :], g_ref[...]).astype(BF16)

    def proj(r):
        return jnp.dot(a_scr[pl.ds(r * rt, rt), :], w_ref[...], preferred_element_type=F32)

    n_qk_steps = 2 * RET_HEADS * RET_QK_DIM // RET_IN_CHUNK
    half = RET_QK_DIM // 2

    @pl.when(j < n_qk_steps)
    def _():
        scale = jnp.where(j >= n_qk_steps // 2, RET_QK_DIM ** -0.5, 1.0).astype(F32)
        for r in range(nrt):
            rows = pl.ds(r * rt, rt)
            p = proj(r)
            c = cos_ref[rows, :]
            s = sin_ref[rows, :]
            for hh in range(RET_IN_CHUNK // RET_QK_DIM):
                lo = hh * RET_QK_DIM
                x1 = p[:, lo:lo + half]
                x2 = p[:, lo + half:lo + RET_QK_DIM]
                p_ref[0, rows, lo:lo + half] = ((x1 * c - x2 * s) * scale).astype(BF16)
                p_ref[0, rows, lo + half:lo + RET_QK_DIM] = ((x2 * c + x1 * s) * scale).astype(BF16)

    @pl.when(j >= n_qk_steps)
    def _():
        for r in range(nrt):
            p_ref[0, pl.ds(r * rt, rt), :] = proj(r).astype(BF16)


def _ret_in(x, meta, gain, w_in, cos, sin):
    b, seq, d = x.shape
    n_meta = meta.shape[0]
    t = n_meta + seq
    n_out = w_in.shape[1]
    rt = _row_tile(t)
    nrt = t // rt
    half = RET_QK_DIM // 2
    return pl.pallas_call(
        functools.partial(_ret_in_kernel, rt=rt, nrt=nrt),
        grid=(b, n_out // RET_IN_CHUNK),
        in_specs=[
            pl.BlockSpec((1, seq, d), lambda i, j: (i, 0, 0)),
            pl.BlockSpec((n_meta, d), lambda i, j: (0, 0)),
            pl.BlockSpec((1, d), lambda i, j: (0, 0)),
            pl.BlockSpec((d, RET_IN_CHUNK), lambda i, j: (0, j)),
            pl.BlockSpec((t, half), lambda i, j: (0, 0)),
            pl.BlockSpec((t, half), lambda i, j: (0, 0)),
        ],
        out_specs=pl.BlockSpec((1, t, RET_IN_CHUNK), lambda i, j: (i, 0, j)),
        out_shape=jax.ShapeDtypeStruct((b, t, n_out), BF16),
        scratch_shapes=[pltpu.VMEM((t, d), BF16)],
        compiler_params=_params(("arbitrary", "arbitrary")),
        name="ret_in_proj",
    )(x, meta, gain.reshape(1, d), w_in, cos, sin)


RET_HEADS_PER_STEP = 2


def _retention_kernel(gc_ref, q_ref, k_ref, v_ref, g_ref, dm_ref, qf_ref, qb_ref, kf_ref, kb_ref,
                      y_ref, rf_scr, rb_scr, rbs_scr, *, n_full, lead):
    c = RET_CHUNK
    dk, dv = RET_QK_DIM, RET_V_DIM
    heads = range(RET_HEADS_PER_STEP)
    gcs = [gc_ref[pl.program_id(1) * RET_HEADS_PER_STEP + i] for i in heads]
    contract_rows = (((0,), (0,)), ((), ()))
    contract_feat = (((1,), (1,)), ((), ()))
    qcol = lambda i: slice(i * dk, (i + 1) * dk)
    vcol = lambda i: slice(i * dv, (i + 1) * dv)

    def state_update(r_scr, i, rows, decay, first):
        kd = k_ref[0, rows, qcol(i)] * decay
        kv = lax.dot_general(kd, v_ref[0, rows, vcol(i)], contract_rows, preferred_element_type=F32)
        r_scr[i] = kv if first else gcs[i] * r_scr[i] + kv

    all_t = pl.ds(0, c)
    lead_rows = pl.ds(0, lead)
    lead_t = pl.ds(c - lead, lead)
    full_rows = lambda n: pl.ds(lead + n * c, c)
    lead_slot = n_full - 1

    for n in reversed(range(n_full)):
        first = n == n_full - 1
        if not first:
            for i in heads:
                rbs_scr[i, n] = rb_scr[i].astype(BF16)
        for i in heads:
            state_update(rb_scr, i, full_rows(n), kb_ref[i, all_t, :], first)
    for i in heads:
        rbs_scr[i, lead_slot] = rb_scr[i].astype(BF16)

    def fwd_chunk(rows, trows, rb_slot, first, update):
        qs = [q_ref[0, rows, qcol(i)] for i in heads]
        ss = [lax.dot_general(qs[i], k_ref[0, rows, qcol(i)], contract_feat,
                              preferred_element_type=F32) for i in heads]
        ps = [(ss[i] * dm_ref[i, trows, trows]).astype(BF16) for i in heads]
        os_ = []
        for i in heads:
            o = jnp.dot(ps[i], v_ref[0, rows, vcol(i)], preferred_element_type=F32)
            if not first:
                o = o + jnp.dot(qs[i] * qf_ref[i, trows, :], rf_scr[i].astype(BF16),
                                preferred_element_type=F32)
            if rb_slot is not None:
                o = o + jnp.dot(qs[i] * qb_ref[i, trows, :], rbs_scr[i, rb_slot],
                                preferred_element_type=F32)
            os_.append(o)
        if update:
            for i in heads:
                state_update(rf_scr, i, rows, kf_ref[i, trows, :], first)
        for i in heads:
            o = os_[i]
            mu = jnp.mean(o, axis=-1, keepdims=True)
            oc = o - mu
            var = jnp.mean(oc * oc, axis=-1, keepdims=True)
            on = oc * lax.rsqrt(var + RMS_EPS)
            hg = 0.5 * g_ref[0, rows, vcol(i)]
            y_ref[0, rows, vcol(i)] = (hg * jnp.tanh(hg) + hg) * on.astype(BF16)

    fwd_chunk(lead_rows, lead_t, lead_slot, first=True, update=True)
    for n in range(n_full):
        last = n == n_full - 1
        fwd_chunk(full_rows(n), all_t, None if last else n, first=False, update=not last)


def _retention_tables():
    c = RET_CHUNK
    log_gamma = jnp.log1p(-jnp.exp2(-5.0 - jnp.arange(RET_HEADS, dtype=F32)))
    idx = jnp.arange(c, dtype=F32)
    lg = log_gamma[:, None, None]
    dmat = jnp.exp(lg * jnp.abs(idx[:, None] - idx[None, :])[None])
    qf = jnp.exp(log_gamma[:, None] * (idx + 1.0)[None])[..., None]
    qb = jnp.exp(log_gamma[:, None] * (c - idx)[None])[..., None]
    kf = jnp.exp(log_gamma[:, None] * (c - 1.0 - idx)[None])[..., None]
    kb = jnp.exp(log_gamma[:, None] * idx[None])[..., None]
    gc = jnp.exp(log_gamma * c)
    bk = lambda x: jnp.broadcast_to(x, (RET_HEADS, c, RET_QK_DIM)).astype(BF16)
    return gc, dmat, bk(qf), bk(qb), bk(kf), bk(kb)


def _retention(p):
    b, t, _ = p.shape
    c = RET_CHUNK
    lead = t % c
    n_full = t // c
    assert lead % BF16_ROWS == 0 and lead > 0
    gc, dmat, qf, qb, kf, kb = _retention_tables()
    dk, dv, nh = RET_QK_DIM, RET_V_DIM, RET_HEADS
    hp = RET_HEADS_PER_STEP
    ngrp = nh // hp
    k_off = nh * dk // (hp * dk)
    v_off = 2 * nh * dk // (hp * dv)
    g_off = v_off + ngrp
    tab = lambda w: pl.BlockSpec((hp, c, w), lambda i, h: (h, 0, 0))
    return pl.pallas_call(
        functools.partial(_retention_kernel, n_full=n_full, lead=lead),
        grid=(b, ngrp),
        in_specs=[
            pl.BlockSpec(memory_space=pltpu.SMEM),
            pl.BlockSpec((1, t, hp * dk), lambda i, h: (i, 0, h)),
            pl.BlockSpec((1, t, hp * dk), lambda i, h: (i, 0, k_off + h)),
            pl.BlockSpec((1, t, hp * dv), lambda i, h: (i, 0, v_off + h)),
            pl.BlockSpec((1, t, hp * dv), lambda i, h: (i, 0, g_off + h)),
            tab(c), tab(dk), tab(dk), tab(dk), tab(dk),
        ],
        out_specs=pl.BlockSpec((1, t, hp * dv), lambda i, h: (i, 0, h)),
        out_shape=jax.ShapeDtypeStruct((b, t, nh * dv), BF16),
        scratch_shapes=[
            pltpu.VMEM((hp, dk, dv), F32),
            pltpu.VMEM((hp, dk, dv), F32),
            pltpu.VMEM((hp, n_full, dk, dv), BF16),
        ],
        compiler_params=_params(("arbitrary", "arbitrary")),
        name="retention",
    )(gc, p, p, p, p, dmat, qf, qb, kf, kb)


def _out_proj_kernel(y_ref, w_ref, h_ref, g_ref, o_ref):
    m = jnp.dot(y_ref[0], w_ref[...], preferred_element_type=F32)
    o_ref[0] = h_ref[0] + _rms(m, g_ref[...])


def _out_proj_first_kernel(y_ref, w_ref, x_ref, meta_ref, g_ref, o_ref):
    n_meta = meta_ref.shape[0]
    rt = o_ref.shape[1]
    res = _rms(jnp.dot(y_ref[0], w_ref[...], preferred_element_type=F32), g_ref[...])

    @pl.when(pl.program_id(1) == 0)
    def _():
        o_ref[0, 0:n_meta, :] = meta_ref[...] + res[0:n_meta, :]
        o_ref[0, n_meta:, :] = x_ref[0, 0:rt - n_meta, :] + res[n_meta:, :]

    @pl.when(pl.program_id(1) != 0)
    def _():
        o_ref[0] = x_ref[0] + res


def _out_proj_first(y, w_out, x, meta, gain):
    b, seq, d = x.shape
    n_meta = meta.shape[0]
    t = n_meta + seq
    kdim = y.shape[-1]
    rt = _row_tile(t)
    return pl.pallas_call(
        _out_proj_first_kernel,
        grid=(b, t // rt),
        in_specs=[
            pl.BlockSpec((1, rt, kdim), lambda i, r: (i, r, 0)),
            pl.BlockSpec((kdim, d), lambda i, r: (0, 0)),
            pl.BlockSpec((pl.Element(1), pl.Element(rt), pl.Element(d)),
                         lambda i, r: (i, jnp.maximum(r * (rt // SUBLANES) - n_meta // SUBLANES, 0)
                                       * SUBLANES, 0)),
            pl.BlockSpec((n_meta, d), lambda i, r: (0, 0)),
            pl.BlockSpec((1, d), lambda i, r: (0, 0)),
        ],
        out_specs=pl.BlockSpec((1, rt, d), lambda i, r: (i, r, 0)),
        out_shape=jax.ShapeDtypeStruct((b, t, d), F32),
        compiler_params=_params(("arbitrary", "arbitrary")),
        name="out_proj_first",
    )(y, w_out, x, meta, gain.reshape(1, d))


def _out_proj(y, w_out, h, gain):
    b, t, d = h.shape
    kdim = y.shape[-1]
    rt = _row_tile(t)
    return pl.pallas_call(
        _out_proj_kernel,
        grid=(b, t // rt),
        in_specs=[
            pl.BlockSpec((1, rt, kdim), lambda i, r: (i, r, 0)),
            pl.BlockSpec((kdim, d), lambda i, r: (0, 0)),
            pl.BlockSpec((1, rt, d), lambda i, r: (i, r, 0)),
            pl.BlockSpec((1, d), lambda i, r: (0, 0)),
        ],
        out_specs=pl.BlockSpec((1, rt, d), lambda i, r: (i, r, 0)),
        out_shape=jax.ShapeDtypeStruct((b, t, d), F32),
        compiler_params=_params(("arbitrary", "arbitrary")),
        name="out_proj",
    )(y, w_out, h, gain.reshape(1, d))


FFN_CHUNK = 1024
FFN_SUB = 256
FFN_HALO = BF16_ROWS


def _ffn_core(wv_ref, wg_ref, cwv_ref, cwg_ref, cbv_ref, cbg_ref, wo_ref, o_ref, a_scr, uv_scr,
              ug_scr, *, rt, fc):
    halo = FFN_HALO
    a = a_scr[...]
    spc = FFN_SUB // LANES
    nsub = fc // FFN_SUB

    def up_proj(c):
        cols = slice(c * FFN_SUB, (c + 1) * FFN_SUB)
        uv = jnp.dot(a, wv_ref[:, cols], preferred_element_type=F32)
        ug = jnp.dot(a, wg_ref[:, cols], preferred_element_type=F32)
        for s in range(spc):
            uv_scr[c * spc + s] = uv[:, s * LANES:(s + 1) * LANES]
            ug_scr[c * spc + s] = ug[:, s * LANES:(s + 1) * LANES]

    def conv(u_scr, c, cw_ref, cb_ref):
        outs = []
        for s in range(spc):
            lanes = slice(c * FFN_SUB + s * LANES, c * FFN_SUB + (s + 1) * LANES)
            acc = cb_ref[:, lanes]
            for k in range(FFN_CONV_W):
                lo = halo - FFN_CONV_LEFT + k
                acc = acc + cw_ref[k:k + 1, lanes] * u_scr[c * spc + s, lo:lo + rt, :]
            outs.append(acc)
        return jnp.concatenate(outs, axis=1)

    acc = None
    up_proj(0)
    for c in range(nsub):
        if c + 1 < nsub:
            up_proj(c + 1)
        val = conv(uv_scr, c, cwv_ref, cbv_ref)
        gate = conv(ug_scr, c, cwg_ref, cbg_ref)
        act = (_gelu_tanh(gate) * val).astype(BF16)
        part = jnp.dot(act, wo_ref[c * FFN_SUB:(c + 1) * FFN_SUB, :], preferred_element_type=F32)
        acc = part if acc is None else acc + part
    o_ref[0] += acc


def _ffn_kernel(hp_ref, h_ref, hn_ref, g_ref, wv_ref, wg_ref, cwv_ref, cwg_ref, cbv_ref, cbg_ref,
                wo_ref, o_ref, a_scr, uv_scr, ug_scr, *, rt, nrt, nj, fc):
    r = pl.program_id(1)
    j = pl.program_id(2)
    halo = FFN_HALO

    @pl.when(j == 0)
    def _():
        g_pre = g_ref[0:1, :]
        a_scr[halo:halo + rt, :] = _rms(h_ref[0], g_pre).astype(BF16)
        prev = jnp.where(r > 0, _rms(hp_ref[0], g_pre), 0.0)
        a_scr[0:halo, :] = prev.astype(BF16)
        nxt = jnp.where(r < nrt - 1, _rms(hn_ref[0], g_pre), 0.0)
        a_scr[halo + rt:halo + rt + halo, :] = nxt.astype(BF16)
        o_ref[...] = jnp.zeros_like(o_ref)

    _ffn_core(wv_ref, wg_ref, cwv_ref, cwg_ref, cbv_ref, cbg_ref, wo_ref, o_ref, a_scr, uv_scr,
              ug_scr, rt=rt, fc=fc)

    @pl.when(j == nj - 1)
    def _():
        o_ref[0] = h_ref[0] + _rms(o_ref[0], g_ref[1:2, :])


def _ffn_last_kernel(hw_ref, g_ref, wv_ref, wg_ref, cwv_ref, cwg_ref, cbv_ref, cbg_ref,
                     wo_ref, o_ref, a_scr, uv_scr, ug_scr, *, rt, nrt, nj, fc):
    r = pl.program_id(1)
    j = pl.program_id(2)
    halo = FFN_HALO
    is_last = r == nrt - 1
    g_pre = g_ref[0:1, :]
    g_post = g_ref[1:2, :]

    @pl.when(j == 0)
    def _():
        o_ref[...] = jnp.zeros_like(o_ref)

    @pl.when(jnp.logical_and(j == 0, jnp.logical_not(is_last)))
    def _():
        a_scr[...] = _rms(hw_ref[0], g_pre).astype(BF16)

    @pl.when(jnp.logical_and(j == 0, is_last))
    def _():
        a_scr[0:rt, :] = _rms(hw_ref[0, 2 * halo:, :], g_pre).astype(BF16)
        a_scr[rt:, :] = jnp.zeros((2 * halo, a_scr.shape[1]), BF16)

    _ffn_core(wv_ref, wg_ref, cwv_ref, cwg_ref, cbv_ref, cbg_ref, wo_ref, o_ref, a_scr, uv_scr,
              ug_scr, rt=rt, fc=fc)

    @pl.when(jnp.logical_and(j == nj - 1, jnp.logical_not(is_last)))
    def _():
        o_ref[0] = hw_ref[0, halo:halo + rt, :] + _rms(o_ref[0], g_post)

    @pl.when(jnp.logical_and(j == nj - 1, is_last))
    def _():
        o_ref[0, 0:rt - halo, :] = hw_ref[0, 3 * halo:, :] + _rms(o_ref[0, 0:rt - halo, :], g_post)


def _ffn(h, gains, layer, w_in, conv_w, conv_b, w_out, drop_meta=False):
    b, t, d = h.shape
    f = w_out.shape[1]
    rt = _row_tile(t)
    nrt = t // rt
    fc = FFN_CHUNK
    nj = f // fc
    halo = FFN_HALO
    hb = rt // halo
    last = t // halo - 1
    weight_specs = [
        pl.BlockSpec((2, d), lambda i, r, j: (0, 0)),
        pl.BlockSpec((None, d, fc), lambda i, r, j: (layer, 0, j)),
        pl.BlockSpec((None, d, fc), lambda i, r, j: (layer, 0, nj + j)),
        pl.BlockSpec((None, FFN_CONV_W, fc), lambda i, r, j: (layer, 0, j)),
        pl.BlockSpec((None, FFN_CONV_W, fc), lambda i, r, j: (layer, 0, nj + j)),
        pl.BlockSpec((None, 1, fc), lambda i, r, j: (layer, 0, j)),
        pl.BlockSpec((None, 1, fc), lambda i, r, j: (layer, 0, nj + j)),
        pl.BlockSpec((None, fc, d), lambda i, r, j: (layer, j, 0)),
    ]
    conv_b = conv_b.reshape(conv_b.shape[0], 1, 2 * f)
    weights = (gains, w_in, w_in, conv_w, conv_w, conv_b, conv_b, w_out)
    if drop_meta:
        assert halo == N_META
        win = rt + 2 * halo
        max_start = (t - win) // SUBLANES
        body = _ffn_last_kernel
        act_specs = [pl.BlockSpec(
            (pl.Element(1), pl.Element(win), pl.Element(d)),
            lambda i, r, j: (i, jnp.minimum(r * (rt // SUBLANES), max_start) * SUBLANES, 0))]
        acts = (h,)
        out_rows = t - halo
    else:
        body = _ffn_kernel
        act_specs = [
            pl.BlockSpec((1, halo, d), lambda i, r, j: (i, jnp.maximum(r * hb - 1, 0), 0)),
            pl.BlockSpec((1, rt, d), lambda i, r, j: (i, r, 0)),
            pl.BlockSpec((1, halo, d), lambda i, r, j: (i, jnp.minimum((r + 1) * hb, last), 0)),
        ]
        acts = (h, h, h)
        out_rows = t
    return pl.pallas_call(
        functools.partial(body, rt=rt, nrt=nrt, nj=nj, fc=fc),
        grid=(b, nrt, nj),
        in_specs=act_specs + weight_specs,
        out_specs=pl.BlockSpec((1, rt, d), lambda i, r, j: (i, r, 0)),
        out_shape=jax.ShapeDtypeStruct((b, out_rows, d), F32),
        scratch_shapes=[
            pltpu.VMEM((rt + 2 * halo, d), BF16),
            pltpu.VMEM((fc // LANES, rt + 2 * halo, LANES), F32),
            pltpu.VMEM((fc // LANES, rt + 2 * halo, LANES), F32),
        ],
        compiler_params=_params(("arbitrary", "arbitrary", "arbitrary")),
        name="conv_ffn_last" if drop_meta else "conv_ffn",
    )(*acts, *weights)


LRU_PAD = SUBLANES
LRU_SCAN_UNROLL = 6


def _lru_kernel(h_ref, g_ref, wgate_ref, wx_ref, cw_ref, cb_ref, wa_ref, ba_ref, wi_ref, bi_ref,
                lam_ref, y_ref, a_scr, xp_scr, xc_scr, gate_scr, av_scr, u_scr, hs_scr,
                *, rt, nrt, t):
    nb = pl.program_id(1)
    w = LRU_BLOCK_W
    nslab = w // LANES
    seg = t // SUBLANES
    pad = LRU_PAD
    lanes = lambda s: slice(s * LANES, (s + 1) * LANES)

    @pl.when(nb == 0)
    def _():
        for r in range(nrt):
            rows = pl.ds(r * rt, rt)
            a_scr[rows, :] = _rms(h_ref[0, rows, :], g_ref[...]).astype(BF16)

    for s in range(nslab):
        xp_scr[s, 0:pad, :] = jnp.zeros((pad, LANES), F32)
        xp_scr[s, pad + t:pad + t + pad, :] = jnp.zeros((pad, LANES), F32)

    def in_proj(r):
        a = a_scr[pl.ds(r * rt, rt), :]
        xr = jnp.dot(a, wx_ref[...], preferred_element_type=F32)
        for s in range(nslab):
            xp_scr[s, pl.ds(pad + r * rt, rt), :] = xr[:, lanes(s)]
        gate_scr[pl.ds(r * rt, rt), :] = jnp.dot(a, wgate_ref[...], preferred_element_type=F32)

    def conv(r):
        outs = []
        for s in range(nslab):
            acc = cb_ref[:, lanes(s)]
            for k in range(LRU_CONV_W):
                lo = pad + r * rt - LRU_CONV_LEFT + k
                acc = acc + cw_ref[k:k + 1, lanes(s)] * xp_scr[s, lo:lo + rt, :]
            outs.append(acc)
        xc_scr[pl.ds(r * rt, rt), :] = jnp.concatenate(outs, axis=1)

    def gate_dots(r):
        xb = xc_scr[pl.ds(r * rt, rt), :].astype(BF16)
        return [(jnp.dot(xb, wa_ref[d, 0], preferred_element_type=F32),
                 jnp.dot(xb, wi_ref[d, 0], preferred_element_type=F32)) for d in range(2)]

    def gate_elementwise(r, raw):
        rows = pl.ds(r * rt, rt)
        x = xc_scr[rows, :]
        hx = 0.5 * x
        for d in range(2):
            ga_raw, gi_raw = raw[d]
            lam = lam_ref[d:d + 1, :]
            softplus_neg = jnp.maximum(-lam, 0.0) + jnp.log1p(jnp.exp(-jnp.abs(lam)))
            c1 = (-0.5 * LRU_C) * softplus_neg
            t_a = jnp.tanh(0.5 * ga_raw + 0.5 * ba_ref[d:d + 1, :])
            log_a = c1 * t_a + c1
            av = jnp.exp(log_a)
            z = jnp.tanh(log_a) * (-1.0 - av * av)
            mult = jnp.where(z > 0.0, z * lax.rsqrt(z), 0.0)
            t_i = jnp.tanh(0.5 * gi_raw + 0.5 * bi_ref[d:d + 1, :])
            uu = mult * (hx * t_i + hx)
            for s in range(nslab):
                av_scr[d, s, rows, :] = av[:, lanes(s)]
                u_scr[d, s, rows, :] = uu[:, lanes(s)]

    in_proj(0)
    raw = None
    for r in range(nrt):
        if r + 1 < nrt:
            in_proj(r + 1)
        conv(r)
        if raw is not None:
            gate_elementwise(r - 1, raw)
        raw = gate_dots(r)
    gate_elementwise(nrt - 1, raw)

    sub = lax.broadcasted_iota(jnp.int32, (SUBLANES, LANES), 0)
    chains = [(d, s) for d in range(2) for s in range(nslab)]

    def strided(d, ii):
        i = ii if d == 0 else seg - 1 - ii
        return pl.ds(i, SUBLANES, stride=seg)

    def pass1(ii, carry):
        out = []
        for (d, s), (hc, pc) in zip(chains, carry):
            a_i = av_scr[d, s, strided(d, ii), :]
            out.append((a_i * hc + u_scr[d, s, strided(d, ii), :], a_i * pc))
        return tuple(out)

    init = tuple((jnp.zeros((SUBLANES, LANES), F32), jnp.ones((SUBLANES, LANES), F32))
                 for _ in chains)
    ends = lax.fori_loop(0, seg, pass1, init, unroll=LRU_SCAN_UNROLL)

    cins = []
    for (d, s), (h_end, p_end) in zip(chains, ends):
        shift = 1 if d == 0 else SUBLANES - 1
        edge = 0 if d == 0 else SUBLANES - 1
        total = h_end
        for _ in range(SUBLANES - 1):
            moved = jnp.where(sub == edge, 0.0, pltpu.roll(total, shift, 0))
            total = h_end + p_end * moved
        cins.append(jnp.where(sub == edge, 0.0, pltpu.roll(total, shift, 0)))

    def pass2(ii, carry):
        out = []
        for (d, s), hc in zip(chains, carry):
            hc = av_scr[d, s, strided(d, ii), :] * hc + u_scr[d, s, strided(d, ii), :]
            hs_scr[d, s, strided(d, ii), :] = hc
            out.append(hc)
        return tuple(out)

    lax.fori_loop(0, seg, pass2, tuple(cins), unroll=LRU_SCAN_UNROLL)

    for r in range(nrt):
        rows = pl.ds(r * rt, rt)
        gl = _gelu_tanh(gate_scr[rows, :])
        for s in range(nslab):
            y_ref[0, rows, lanes(s)] = (
                (hs_scr[0, s, rows, :] + hs_scr[1, s, rows, :]) * gl[:, lanes(s)]).astype(BF16)


def _lru(h, gain, w_in, conv_w, conv_b, w_a, b_a, w_i, b_i, lam):
    b, t, d = h.shape
    r_width = w_in.shape[1] // 2
    w = LRU_BLOCK_W
    nblk = r_width // w
    rt = _row_tile(t)
    nrt = t // rt
    assert t % SUBLANES == 0
    nslab = w // LANES
    vec = lambda rows: pl.BlockSpec((rows, w), lambda i, n: (0, n))
    return pl.pallas_call(
        functools.partial(_lru_kernel, rt=rt, nrt=nrt, t=t),
        grid=(b, nblk),
        in_specs=[
            pl.BlockSpec((1, t, d), lambda i, n: (i, 0, 0)),
            pl.BlockSpec((1, d), lambda i, n: (0, 0)),
            pl.BlockSpec((d, w), lambda i, n: (0, n)),
            pl.BlockSpec((d, w), lambda i, n: (0, nblk + n)),
            vec(LRU_CONV_W), vec(1),
            pl.BlockSpec((2, 1, w, w), lambda i, n: (0, n, 0, 0)), vec(2),
            pl.BlockSpec((2, 1, w, w), lambda i, n: (0, n, 0, 0)), vec(2),
            vec(2),
        ],
        out_specs=pl.BlockSpec((1, t, w), lambda i, n: (i, 0, n)),
        out_shape=jax.ShapeDtypeStruct((b, t, r_width), BF16),
        scratch_shapes=[
            pltpu.VMEM((t, d), BF16),
            pltpu.VMEM((nslab, t + 2 * LRU_PAD, LANES), F32),
            pltpu.VMEM((t, w), F32),
            pltpu.VMEM((t, w), F32),
            pltpu.VMEM((2, nslab, t, LANES), F32),
            pltpu.VMEM((2, nslab, t, LANES), F32),
            pltpu.VMEM((2, nslab, t, LANES), F32),
        ],
        compiler_params=_params(("arbitrary", "arbitrary")),
        name="rglru",
    )(h, gain.reshape(1, d), w_in, w_in, conv_w, conv_b.reshape(1, r_width), w_a, b_a, w_i, b_i, lam)


def kernel(x, meta_tokens, norm_gains, ret_w_in, ret_w_out, lru_w_in, lru_conv_w, lru_conv_b,
           lru_w_a, lru_b_a, lru_w_i, lru_b_i, lru_lambda, lru_w_out, ffn_w_in, ffn_conv_w,
           ffn_conv_b, ffn_w_out):
    meta = meta_tokens.astype(x.dtype)
    t = N_META + x.shape[1]

    pos = jnp.arange(t, dtype=F32)
    inv = ROPE_BASE ** (-jnp.arange(0, RET_QK_DIM, 2, dtype=F32) / RET_QK_DIM)
    ang = pos[:, None] * inv[None, :]
    cos, sin = jnp.cos(ang), jnp.sin(ang)

    ffn_w_in_bf = ffn_w_in.astype(BF16)
    ffn_w_out_bf = ffn_w_out.astype(BF16)

    g = norm_gains[0]
    p = _ret_in(x, meta, g[0], ret_w_in[0].astype(BF16), cos, sin)
    y = _retention(p)
    h = _out_proj_first(y, ret_w_out[0].astype(BF16), x, meta, g[1])
    h = _ffn(h, g[2:4], 0, ffn_w_in_bf, ffn_conv_w, ffn_conv_b, ffn_w_out_bf)

    g = norm_gains[1]
    y = _lru(h, g[0], lru_w_in[0].astype(BF16), lru_conv_w[0], lru_conv_b[0],
             lru_w_a[0].astype(BF16), lru_b_a[0], lru_w_i[0].astype(BF16), lru_b_i[0], lru_lambda[0])
    h = _out_proj(y, lru_w_out[0].astype(BF16), h, g[1])
    return _ffn(h, g[2:4], 1, ffn_w_in_bf, ffn_conv_w, ffn_conv_b, ffn_w_out_bf, drop_meta=True)
```

```python
import functools

import jax
import jax.numpy as jnp
from jax import lax
from jax.experimental import pallas as pl
from jax.experimental.pallas import tpu as pltpu

F32 = jnp.float32
BF16 = jnp.bfloat16

D_MODEL = 1024
N_META = 16
RMS_EPS = 1e-6

RET_HEADS = 4
RET_QK_DIM = D_MODEL // RET_HEADS
RET_V_DIM = 2 * D_MODEL // RET_HEADS
RET_CHUNK = 256
ROPE_BASE = 10000.0

LRU_BLOCKS = 4
LRU_BLOCK_W = D_MODEL // LRU_BLOCKS
LRU_C = 8.0
LRU_CONV_W = 4
LRU_CONV_LEFT = 2

FFN_HIDDEN = 3 * D_MODEL
FFN_CONV_W = 3
FFN_CONV_LEFT = 1

SUBLANES = 8
LANES = 128
BF16_ROWS = 16

VMEM_LIMIT_BYTES = 56 * 1024 * 1024


def _row_tile(t):
    for n in (2, 3, 4, 5, 6, 8):
        if t % n == 0 and (t // n) % BF16_ROWS == 0:
            return t // n
    return t


def _rms(x, g):
    ms = jnp.mean(x * x, axis=-1, keepdims=True)
    return x * lax.rsqrt(ms + RMS_EPS) * g


def _gelu_tanh(x):
    c = 0.7978845608028654
    return 0.5 * x * (1.0 + jnp.tanh(c * (x + 0.044715 * (x * x * x))))


def _params(sem):
    return pltpu.CompilerParams(dimension_semantics=sem, vmem_limit_bytes=VMEM_LIMIT_BYTES)


RET_IN_CHUNK = 1024


def _ret_in_kernel(x_ref, meta_ref, g_ref, w_ref, cos_ref, sin_ref, p_ref, a_scr, *, rt, nrt):
    j = pl.program_id(1)
    n_meta = meta_ref.shape[0]
    seq = x_ref.shape[1]
    xt = _row_tile(seq)

    @pl.when(j == 0)
    def _():
        a_scr[0:n_meta, :] = _rms(meta_ref[...], g_ref[...]).astype(BF16)
        for r in range(seq // xt):
            a_scr[pl.ds(n_meta + r * xt, xt), :] = _rms(
                x_ref[0, pl.ds(r * xt, xt), :], g_ref[...]).astype(BF16)

    def proj(r):
        return jnp.dot(a_scr[pl.ds(r * rt, rt), :], w_ref[...], preferred_element_type=F32)

    n_qk_steps = 2 * RET_HEADS * RET_QK_DIM // RET_IN_CHUNK
    half = RET_QK_DIM // 2

    @pl.when(j < n_qk_steps)
    def _():
        scale = jnp.where(j >= n_qk_steps // 2, RET_QK_DIM ** -0.5, 1.0).astype(F32)
        for r in range(nrt):
            rows = pl.ds(r * rt, rt)
            p = proj(r)
            c = cos_ref[rows, :]
            s = sin_ref[rows, :]
            for hh in range(RET_IN_CHUNK // RET_QK_DIM):
                lo = hh * RET_QK_DIM
                x1 = p[:, lo:lo + half]
                x2 = p[:, lo + half:lo + RET_QK_DIM]
                p_ref[0, rows, lo:lo + half] = ((x1 * c - x2 * s) * scale).astype(BF16)
                p_ref[0, rows, lo + half:lo + RET_QK_DIM] = ((x2 * c + x1 * s) * scale).astype(BF16)

    @pl.when(j >= n_qk_steps)
    def _():
        for r in range(nrt):
            p_ref[0, pl.ds(r * rt, rt), :] = proj(r).astype(BF16)


def _ret_in(x, meta, gain, w_in, cos, sin):
    b, seq, d = x.shape
    n_meta = meta.shape[0]
    t = n_meta + seq
    n_out = w_in.shape[1]
    rt = _row_tile(t)
    nrt = t // rt
    half = RET_QK_DIM // 2
    return pl.pallas_call(
        functools.partial(_ret_in_kernel, rt=rt, nrt=nrt),
        grid=(b, n_out // RET_IN_CHUNK),
        in_specs=[
            pl.BlockSpec((1, seq, d), lambda i, j: (i, 0, 0)),
            pl.BlockSpec((n_meta, d), lambda i, j: (0, 0)),
            pl.BlockSpec((1, d), lambda i, j: (0, 0)),
            pl.BlockSpec((d, RET_IN_CHUNK), lambda i, j: (0, j)),
            pl.BlockSpec((t, half), lambda i, j: (0, 0)),
            pl.BlockSpec((t, half), lambda i, j: (0, 0)),
        ],
        out_specs=pl.BlockSpec((1, t, RET_IN_CHUNK), lambda i, j: (i, 0, j)),
        out_shape=jax.ShapeDtypeStruct((b, t, n_out), BF16),
        scratch_shapes=[pltpu.VMEM((t, d), BF16)],
        compiler_params=_params(("arbitrary", "arbitrary")),
        name="ret_in_proj",
    )(x, meta, gain.reshape(1, d), w_in, cos, sin)


RET_HEADS_PER_STEP = 2


def _retention_kernel(gc_ref, q_ref, k_ref, v_ref, g_ref, dm_ref, qf_ref, qb_ref, kf_ref, kb_ref,
                      y_ref, rf_scr, rb_scr, rbs_scr, *, n_full, lead):
    c = RET_CHUNK
    dk, dv = RET_QK_DIM, RET_V_DIM
    heads = range(RET_HEADS_PER_STEP)
    gcs = [gc_ref[pl.program_id(1) * RET_HEADS_PER_STEP + i] for i in heads]
    contract_rows = (((0,), (0,)), ((), ()))
    contract_feat = (((1,), (1,)), ((), ()))
    qcol = lambda i: slice(i * dk, (i + 1) * dk)
    vcol = lambda i: slice(i * dv, (i + 1) * dv)

    def state_update(r_scr, i, rows, decay, first):
        kd = k_ref[0, rows, qcol(i)] * decay
        kv = lax.dot_general(kd, v_ref[0, rows, vcol(i)], contract_rows, preferred_element_type=F32)
        r_scr[i] = kv if first else gcs[i] * r_scr[i] + kv

    all_t = pl.ds(0, c)
    lead_rows = pl.ds(0, lead)
    lead_t = pl.ds(c - lead, lead)
    full_rows = lambda n: pl.ds(lead + n * c, c)
    lead_slot = n_full - 1

    for n in reversed(range(n_full)):
        first = n == n_full - 1
        if not first:
            for i in heads:
                rbs_scr[i, n] = rb_scr[i].astype(BF16)
        for i in heads:
            state_update(rb_scr, i, full_rows(n), kb_ref[i, all_t, :], first)
    for i in heads:
        rbs_scr[i, lead_slot] = rb_scr[i].astype(BF16)

    def fwd_chunk(rows, trows, rb_slot, first, update):
        qs = [q_ref[0, rows, qcol(i)] for i in heads]
        ss = [lax.dot_general(qs[i], k_ref[0, rows, qcol(i)], contract_feat,
                              preferred_element_type=F32) for i in heads]
        ps = [(ss[i] * dm_ref[i, trows, trows]).astype(BF16) for i in heads]
        os_ = []
        for i in heads:
            o = jnp.dot(ps[i], v_ref[0, rows, vcol(i)], preferred_element_type=F32)
            if not first:
                o = o + jnp.dot(qs[i] * qf_ref[i, trows, :], rf_scr[i].astype(BF16),
                                preferred_element_type=F32)
            if rb_slot is not None:
                o = o + jnp.dot(qs[i] * qb_ref[i, trows, :], rbs_scr[i, rb_slot],
                                preferred_element_type=F32)
            os_.append(o)
        if update:
            for i in heads:
                state_update(rf_scr, i, rows, kf_ref[i, trows, :], first)
        for i in heads:
            o = os_[i]
            mu = jnp.mean(o, axis=-1, keepdims=True)
            oc = o - mu
            var = jnp.mean(oc * oc, axis=-1, keepdims=True)
            on = oc * lax.rsqrt(var + RMS_EPS)
            hg = 0.5 * g_ref[0, rows, vcol(i)]
            y_ref[0, rows, vcol(i)] = (hg * jnp.tanh(hg) + hg) * on.astype(BF16)

    fwd_chunk(lead_rows, lead_t, lead_slot, first=True, update=True)
    for n in range(n_full):
        last = n == n_full - 1
        fwd_chunk(full_rows(n), all_t, None if last else n, first=False, update=not last)


def _retention_tables():
    c = RET_CHUNK
    log_gamma = jnp.log1p(-jnp.exp2(-5.0 - jnp.arange(RET_HEADS, dtype=F32)))
    idx = jnp.arange(c, dtype=F32)
    lg = log_gamma[:, None, None]
    dmat = jnp.exp(lg * jnp.abs(idx[:, None] - idx[None, :])[None])
    qf = jnp.exp(log_gamma[:, None] * (idx + 1.0)[None])[..., None]
    qb = jnp.exp(log_gamma[:, None] * (c - idx)[None])[..., None]
    kf = jnp.exp(log_gamma[:, None] * (c - 1.0 - idx)[None])[..., None]
    kb = jnp.exp(log_gamma[:, None] * idx[None])[..., None]
    gc = jnp.exp(log_gamma * c)
    bk = lambda x: jnp.broadcast_to(x, (RET_HEADS, c, RET_QK_DIM)).astype(BF16)
    return gc, dmat, bk(qf), bk(qb), bk(kf), bk(kb)


def _retention(p):
    b, t, _ = p.shape
    c = RET_CHUNK
    lead = t % c
    n_full = t // c
    assert lead % BF16_ROWS == 0 and lead > 0
    gc, dmat, qf, qb, kf, kb = _retention_tables()
    dk, dv, nh = RET_QK_DIM, RET_V_DIM, RET_HEADS
    hp = RET_HEADS_PER_STEP
    ngrp = nh // hp
    k_off = nh * dk // (hp * dk)
    v_off = 2 * nh * dk // (hp * dv)
    g_off = v_off + ngrp
    tab = lambda w: pl.BlockSpec((hp, c, w), lambda i, h: (h, 0, 0))
    return pl.pallas_call(
        functools.partial(_retention_kernel, n_full=n_full, lead=lead),
        grid=(b, ngrp),
        in_specs=[
            pl.BlockSpec(memory_space=pltpu.SMEM),
            pl.BlockSpec((1, t, hp * dk), lambda i, h: (i, 0, h)),
            pl.BlockSpec((1, t, hp * dk), lambda i, h: (i, 0, k_off + h)),
            pl.BlockSpec((1, t, hp * dv), lambda i, h: (i, 0, v_off + h)),
            pl.BlockSpec((1, t, hp * dv), lambda i, h: (i, 0, g_off + h)),
            tab(c), tab(dk), tab(dk), tab(dk), tab(dk),
        ],
        out_specs=pl.BlockSpec((1, t, hp * dv), lambda i, h: (i, 0, h)),
        out_shape=jax.ShapeDtypeStruct((b, t, nh * dv), BF16),
        scratch_shapes=[
            pltpu.VMEM((hp, dk, dv), F32),
            pltpu.VMEM((hp, dk, dv), F32),
            pltpu.VMEM((hp, n_full, dk, dv), BF16),
        ],
        compiler_params=_params(("arbitrary", "arbitrary")),
        name="retention",
    )(gc, p, p, p, p, dmat, qf, qb, kf, kb)


def _row_pieces(rt):
    cut = -(-(rt // 2) // BF16_ROWS) * BF16_ROWS
    return ((0, cut), (cut, rt - cut))


def _out_proj_pieces(y_ref, w_ref, g_ref):
    pieces = _row_pieces(y_ref.shape[1])
    ms = [jnp.dot(y_ref[0, lo:lo + n, :], w_ref[...], preferred_element_type=F32)
          for lo, n in pieces]
    return [(lo, n, _rms(m, g_ref[...])) for (lo, n), m in zip(pieces, ms)]


def _out_proj_kernel(y_ref, w_ref, h_ref, g_ref, o_ref):
    for lo, n, res in _out_proj_pieces(y_ref, w_ref, g_ref):
        o_ref[0, lo:lo + n, :] = h_ref[0, lo:lo + n, :] + res


def _out_proj_first_kernel(y_ref, w_ref, x_ref, meta_ref, g_ref, o_ref):
    n_meta = meta_ref.shape[0]
    pieces = _out_proj_pieces(y_ref, w_ref, g_ref)

    @pl.when(pl.program_id(1) == 0)
    def _():
        for lo, n, res in pieces:
            if lo == 0:
                o_ref[0, 0:n_meta, :] = meta_ref[...] + res[0:n_meta, :]
                o_ref[0, n_meta:n, :] = x_ref[0, 0:n - n_meta, :] + res[n_meta:, :]
            else:
                o_ref[0, lo:lo + n, :] = x_ref[0, lo - n_meta:lo - n_meta + n, :] + res

    @pl.when(pl.program_id(1) != 0)
    def _():
        for lo, n, res in pieces:
            o_ref[0, lo:lo + n, :] = x_ref[0, lo:lo + n, :] + res


def _out_proj_first(y, w_out, x, meta, gain):
    b, seq, d = x.shape
    n_meta = meta.shape[0]
    t = n_meta + seq
    kdim = y.shape[-1]
    rt = _row_tile(t)
    return pl.pallas_call(
        _out_proj_first_kernel,
        grid=(b, t // rt),
        in_specs=[
            pl.BlockSpec((1, rt, kdim), lambda i, r: (i, r, 0)),
            pl.BlockSpec((kdim, d), lambda i, r: (0, 0)),
            pl.BlockSpec((pl.Element(1), pl.Element(rt), pl.Element(d)),
                         lambda i, r: (i, jnp.maximum(r * (rt // SUBLANES) - n_meta // SUBLANES, 0)
                                       * SUBLANES, 0)),
            pl.BlockSpec((n_meta, d), lambda i, r: (0, 0)),
            pl.BlockSpec((1, d), lambda i, r: (0, 0)),
        ],
        out_specs=pl.BlockSpec((1, rt, d), lambda i, r: (i, r, 0)),
        out_shape=jax.ShapeDtypeStruct((b, t, d), F32),
        compiler_params=_params(("arbitrary", "arbitrary")),
        name="out_proj_first",
    )(y, w_out, x, meta, gain.reshape(1, d))


def _out_proj(y, w_out, h, gain):
    b, t, d = h.shape
    kdim = y.shape[-1]
    rt = _row_tile(t)
    return pl.pallas_call(
        _out_proj_kernel,
        grid=(b, t // rt),
        in_specs=[
            pl.BlockSpec((1, rt, kdim), lambda i, r: (i, r, 0)),
            pl.BlockSpec((kdim, d), lambda i, r: (0, 0)),
            pl.BlockSpec((1, rt, d), lambda i, r: (i, r, 0)),
            pl.BlockSpec((1, d), lambda i, r: (0, 0)),
        ],
        out_specs=pl.BlockSpec((1, rt, d), lambda i, r: (i, r, 0)),
        out_shape=jax.ShapeDtypeStruct((b, t, d), F32),
        compiler_params=_params(("arbitrary", "arbitrary")),
        name="out_proj",
    )(y, w_out, h, gain.reshape(1, d))


FFN_CHUNK = 1024
FFN_SUB = 256
FFN_AHEAD = 2
FFN_HALO = BF16_ROWS


def _ffn_core(wv_ref, wg_ref, cwv_ref, cwg_ref, cbv_ref, cbg_ref, wo_ref, o_ref, a_scr, uv_scr,
              ug_scr, *, rt, fc):
    halo = FFN_HALO
    a = a_scr[...]
    spc = FFN_SUB // LANES
    nsub = fc // FFN_SUB

    def up_proj(c):
        cols = slice(c * FFN_SUB, (c + 1) * FFN_SUB)
        uv = jnp.dot(a, wv_ref[:, cols], preferred_element_type=F32)
        ug = jnp.dot(a, wg_ref[:, cols], preferred_element_type=F32)
        for s in range(spc):
            uv_scr[c * spc + s] = uv[:, s * LANES:(s + 1) * LANES]
            ug_scr[c * spc + s] = ug[:, s * LANES:(s + 1) * LANES]

    def conv(u_scr, c, cw_ref, cb_ref):
        outs = []
        for s in range(spc):
            lanes = slice(c * FFN_SUB + s * LANES, c * FFN_SUB + (s + 1) * LANES)
            acc = cb_ref[:, lanes]
            for k in range(FFN_CONV_W):
                lo = halo - FFN_CONV_LEFT + k
                acc = acc + cw_ref[k:k + 1, lanes] * u_scr[c * spc + s, lo:lo + rt, :]
            outs.append(acc)
        return jnp.concatenate(outs, axis=1)

    acc = None
    for c in range(FFN_AHEAD):
        up_proj(c)
    for c in range(nsub):
        if c + FFN_AHEAD < nsub:
            up_proj(c + FFN_AHEAD)
        val = conv(uv_scr, c, cwv_ref, cbv_ref)
        gate = conv(ug_scr, c, cwg_ref, cbg_ref)
        act = (_gelu_tanh(gate) * val).astype(BF16)
        part = jnp.dot(act, wo_ref[c * FFN_SUB:(c + 1) * FFN_SUB, :], preferred_element_type=F32)
        acc = part if acc is None else acc + part
    o_ref[0] += acc


def _ffn_kernel(hp_ref, h_ref, hn_ref, g_ref, wv_ref, wg_ref, cwv_ref, cwg_ref, cbv_ref, cbg_ref,
                wo_ref, o_ref, a_scr, uv_scr, ug_scr, *, rt, nrt, nj, fc):
    r = pl.program_id(1)
    j = pl.program_id(2)
    halo = FFN_HALO

    @pl.when(j == 0)
    def _():
        g_pre = g_ref[0:1, :]
        a_scr[halo:halo + rt, :] = _rms(h_ref[0], g_pre).astype(BF16)
        prev = jnp.where(r > 0, _rms(hp_ref[0], g_pre), 0.0)
        a_scr[0:halo, :] = prev.astype(BF16)
        nxt = jnp.where(r < nrt - 1, _rms(hn_ref[0], g_pre), 0.0)
        a_scr[halo + rt:halo + rt + halo, :] = nxt.astype(BF16)
        o_ref[...] = jnp.zeros_like(o_ref)

    _ffn_core(wv_ref, wg_ref, cwv_ref, cwg_ref, cbv_ref, cbg_ref, wo_ref, o_ref, a_scr, uv_scr,
              ug_scr, rt=rt, fc=fc)

    @pl.when(j == nj - 1)
    def _():
        o_ref[0] = h_ref[0] + _rms(o_ref[0], g_ref[1:2, :])


def _ffn_last_kernel(hw_ref, g_ref, wv_ref, wg_ref, cwv_ref, cwg_ref, cbv_ref, cbg_ref,
                     wo_ref, o_ref, a_scr, uv_scr, ug_scr, *, rt, nrt, nj, fc):
    r = pl.program_id(1)
    j = pl.program_id(2)
    halo = FFN_HALO
    is_last = r == nrt - 1
    g_pre = g_ref[0:1, :]
    g_post = g_ref[1:2, :]

    @pl.when(j == 0)
    def _():
        o_ref[...] = jnp.zeros_like(o_ref)

    @pl.when(jnp.logical_and(j == 0, jnp.logical_not(is_last)))
    def _():
        a_scr[...] = _rms(hw_ref[0], g_pre).astype(BF16)

    @pl.when(jnp.logical_and(j == 0, is_last))
    def _():
        a_scr[0:rt, :] = _rms(hw_ref[0, 2 * halo:, :], g_pre).astype(BF16)
        a_scr[rt:, :] = jnp.zeros((2 * halo, a_scr.shape[1]), BF16)

    _ffn_core(wv_ref, wg_ref, cwv_ref, cwg_ref, cbv_ref, cbg_ref, wo_ref, o_ref, a_scr, uv_scr,
              ug_scr, rt=rt, fc=fc)

    @pl.when(jnp.logical_and(j == nj - 1, jnp.logical_not(is_last)))
    def _():
        o_ref[0] = hw_ref[0, halo:halo + rt, :] + _rms(o_ref[0], g_post)

    @pl.when(jnp.logical_and(j == nj - 1, is_last))
    def _():
        o_ref[0, 0:rt - halo, :] = hw_ref[0, 3 * halo:, :] + _rms(o_ref[0, 0:rt - halo, :], g_post)


def _ffn(h, gains, layer, w_in, conv_w, conv_b, w_out, drop_meta=False):
    b, t, d = h.shape
    f = w_out.shape[1]
    rt = _row_tile(t)
    nrt = t // rt
    fc = FFN_CHUNK
    nj = f // fc
    halo = FFN_HALO
    hb = rt // halo
    last = t // halo - 1
    weight_specs = [
        pl.BlockSpec((2, d), lambda i, r, j: (0, 0)),
        pl.BlockSpec((None, d, fc), lambda i, r, j: (layer, 0, j)),
        pl.BlockSpec((None, d, fc), lambda i, r, j: (layer, 0, nj + j)),
        pl.BlockSpec((None, FFN_CONV_W, fc), lambda i, r, j: (layer, 0, j)),
        pl.BlockSpec((None, FFN_CONV_W, fc), lambda i, r, j: (layer, 0, nj + j)),
        pl.BlockSpec((None, 1, fc), lambda i, r, j: (layer, 0, j)),
        pl.BlockSpec((None, 1, fc), lambda i, r, j: (layer, 0, nj + j)),
        pl.BlockSpec((None, fc, d), lambda i, r, j: (layer, j, 0)),
    ]
    conv_b = conv_b.reshape(conv_b.shape[0], 1, 2 * f)
    weights = (gains, w_in, w_in, conv_w, conv_w, conv_b, conv_b, w_out)
    if drop_meta:
        assert halo == N_META
        win = rt + 2 * halo
        max_start = (t - win) // SUBLANES
        body = _ffn_last_kernel
        act_specs = [pl.BlockSpec(
            (pl.Element(1), pl.Element(win), pl.Element(d)),
            lambda i, r, j: (i, jnp.minimum(r * (rt // SUBLANES), max_start) * SUBLANES, 0))]
        acts = (h,)
        out_rows = t - halo
    else:
        body = _ffn_kernel
        act_specs = [
            pl.BlockSpec((1, halo, d), lambda i, r, j: (i, jnp.maximum(r * hb - 1, 0), 0)),
            pl.BlockSpec((1, rt, d), lambda i, r, j: (i, r, 0)),
            pl.BlockSpec((1, halo, d), lambda i, r, j: (i, jnp.minimum((r + 1) * hb, last), 0)),
        ]
        acts = (h, h, h)
        out_rows = t
    return pl.pallas_call(
        functools.partial(body, rt=rt, nrt=nrt, nj=nj, fc=fc),
        grid=(b, nrt, nj),
        in_specs=act_specs + weight_specs,
        out_specs=pl.BlockSpec((1, rt, d), lambda i, r, j: (i, r, 0)),
        out_shape=jax.ShapeDtypeStruct((b, out_rows, d), F32),
        scratch_shapes=[
            pltpu.VMEM((rt + 2 * halo, d), BF16),
            pltpu.VMEM((fc // LANES, rt + 2 * halo, LANES), F32),
            pltpu.VMEM((fc // LANES, rt + 2 * halo, LANES), F32),
        ],
        compiler_params=_params(("arbitrary", "arbitrary", "arbitrary")),
        name="conv_ffn_last" if drop_meta else "conv_ffn",
    )(*acts, *weights)


LRU_PAD = SUBLANES
LRU_SCAN_UNROLL = 6


def _lru_kernel(h_ref, g_ref, wgate_ref, wx_ref, cw_ref, cb_ref, wa_ref, ba_ref, wi_ref, bi_ref,
                lam_ref, y_ref, a_scr, xp_scr, xc_scr, gate_scr, av_scr, u_scr, hs_scr,
                *, rt, nrt, t):
    nb = pl.program_id(1)
    w = LRU_BLOCK_W
    nslab = w // LANES
    seg = t // SUBLANES
    pad = LRU_PAD
    lanes = lambda s: slice(s * LANES, (s + 1) * LANES)

    @pl.when(nb == 0)
    def _():
        for r in range(nrt):
            rows = pl.ds(r * rt, rt)
            a_scr[rows, :] = _rms(h_ref[0, rows, :], g_ref[...]).astype(BF16)

    for s in range(nslab):
        xp_scr[s, 0:pad, :] = jnp.zeros((pad, LANES), F32)
        xp_scr[s, pad + t:pad + t + pad, :] = jnp.zeros((pad, LANES), F32)

    def in_proj(r):
        ext = BF16_ROWS if r + 1 < nrt else 0
        xr = jnp.dot(a_scr[pl.ds(r * rt, rt + ext), :], wx_ref[...], preferred_element_type=F32)
        for s in range(nslab):
            xp_scr[s, pl.ds(pad + r * rt, rt + ext), :] = xr[:, lanes(s)]
        gate = jnp.dot(a_scr[pl.ds(r * rt, rt), :], wgate_ref[...], preferred_element_type=F32)
        gate_scr[pl.ds(r * rt, rt), :] = _gelu_tanh(gate).astype(BF16)

    def conv(r):
        outs = []
        for s in range(nslab):
            acc = cb_ref[:, lanes(s)]
            for k in range(LRU_CONV_W):
                lo = pad + r * rt - LRU_CONV_LEFT + k
                acc = acc + cw_ref[k:k + 1, lanes(s)] * xp_scr[s, lo:lo + rt, :]
            outs.append(acc)
        xc_scr[pl.ds(r * rt, rt), :] = jnp.concatenate(outs, axis=1)

    def gate_dots(r):
        hxb = (0.5 * xc_scr[pl.ds(r * rt, rt), :]).astype(BF16)
        return [(jnp.dot(hxb, wa_ref[d, 0], preferred_element_type=F32),
                 jnp.dot(hxb, wi_ref[d, 0], preferred_element_type=F32)) for d in range(2)]

    def gate_elementwise(r, raw):
        rows = pl.ds(r * rt, rt)
        x = xc_scr[rows, :]
        hx = 0.5 * x
        for d in range(2):
            ga_half, gi_half = raw[d]
            lam = lam_ref[d:d + 1, :]
            softplus_neg = jnp.maximum(-lam, 0.0) + jnp.log1p(jnp.exp(-jnp.abs(lam)))
            c1 = (-0.5 * LRU_C) * softplus_neg
            t_a = jnp.tanh(ga_half + 0.5 * ba_ref[d:d + 1, :])
            log_a = c1 * t_a + c1
            av = jnp.exp(log_a)
            z = jnp.tanh(log_a) * (-1.0 - av * av)
            mult = jnp.where(z > 0.0, z * lax.rsqrt(z), 0.0)
            t_i = jnp.tanh(gi_half + 0.5 * bi_ref[d:d + 1, :])
            uu = mult * (hx * t_i + hx)
            for s in range(nslab):
                av_scr[d, s, rows, :] = av[:, lanes(s)]
                u_scr[d, s, rows, :] = uu[:, lanes(s)]

    in_proj(0)
    for r in range(nrt):
        conv(r)
        raw = gate_dots(r)
        if r + 1 < nrt:
            in_proj(r + 1)
        gate_elementwise(r, raw)

    sub = lax.broadcasted_iota(jnp.int32, (SUBLANES, LANES), 0)
    chains = [(d, s) for d in range(2) for s in range(nslab)]

    def strided(d, ii):
        i = ii if d == 0 else seg - 1 - ii
        return pl.ds(i, SUBLANES, stride=seg)

    def pass1(ii, carry):
        out = []
        for (d, s), (hc, pc) in zip(chains, carry):
            a_i = av_scr[d, s, strided(d, ii), :]
            out.append((a_i * hc + u_scr[d, s, strided(d, ii), :], a_i * pc))
        return tuple(out)

    init = tuple((jnp.zeros((SUBLANES, LANES), F32), jnp.ones((SUBLANES, LANES), F32))
                 for _ in chains)
    ends = lax.fori_loop(0, seg, pass1, init, unroll=LRU_SCAN_UNROLL)

    cins = []
    for (d, s), (h_end, p_end) in zip(chains, ends):
        shift = 1 if d == 0 else SUBLANES - 1
        edge = 0 if d == 0 else SUBLANES - 1
        total = h_end
        for _ in range(SUBLANES - 1):
            moved = jnp.where(sub == edge, 0.0, pltpu.roll(total, shift, 0))
            total = h_end + p_end * moved
        cins.append(jnp.where(sub == edge, 0.0, pltpu.roll(total, shift, 0)))

    def pass2(ii, carry):
        out = []
        for (d, s), hc in zip(chains, carry):
            hc = av_scr[d, s, strided(d, ii), :] * hc + u_scr[d, s, strided(d, ii), :]
            hs_scr[d, s, strided(d, ii), :] = hc
            out.append(hc)
        return tuple(out)

    lax.fori_loop(0, seg, pass2, tuple(cins), unroll=LRU_SCAN_UNROLL)

    for r in range(nrt):
        rows = pl.ds(r * rt, rt)
        gl = gate_scr[rows, :].astype(F32)
        for s in range(nslab):
            y_ref[0, rows, lanes(s)] = (
                (hs_scr[0, s, rows, :] + hs_scr[1, s, rows, :]) * gl[:, lanes(s)]).astype(BF16)


def _lru(h, gain, w_in, conv_w, conv_b, w_a, b_a, w_i, b_i, lam):
    b, t, d = h.shape
    r_width = w_in.shape[1] // 2
    w = LRU_BLOCK_W
    nblk = r_width // w
    rt = _row_tile(t)
    nrt = t // rt
    assert t % SUBLANES == 0
    nslab = w // LANES
    vec = lambda rows: pl.BlockSpec((rows, w), lambda i, n: (0, n))
    return pl.pallas_call(
        functools.partial(_lru_kernel, rt=rt, nrt=nrt, t=t),
        grid=(b, nblk),
        in_specs=[
            pl.BlockSpec((1, t, d), lambda i, n: (i, 0, 0)),
            pl.BlockSpec((1, d), lambda i, n: (0, 0)),
            pl.BlockSpec((d, w), lambda i, n: (0, n)),
            pl.BlockSpec((d, w), lambda i, n: (0, nblk + n)),
            vec(LRU_CONV_W), vec(1),
            pl.BlockSpec((2, 1, w, w), lambda i, n: (0, n, 0, 0)), vec(2),
            pl.BlockSpec((2, 1, w, w), lambda i, n: (0, n, 0, 0)), vec(2),
            vec(2),
        ],
        out_specs=pl.BlockSpec((1, t, w), lambda i, n: (i, 0, n)),
        out_shape=jax.ShapeDtypeStruct((b, t, r_width), BF16),
        scratch_shapes=[
            pltpu.VMEM((t, d), BF16),
            pltpu.VMEM((nslab, t + 2 * LRU_PAD, LANES), F32),
            pltpu.VMEM((t, w), F32),
            pltpu.VMEM((t, w), BF16),
            pltpu.VMEM((2, nslab, t, LANES), F32),
            pltpu.VMEM((2, nslab, t, LANES), F32),
            pltpu.VMEM((2, nslab, t, LANES), F32),
        ],
        compiler_params=_params(("arbitrary", "arbitrary")),
        name="rglru",
    )(h, gain.reshape(1, d), w_in, w_in, conv_w, conv_b.reshape(1, r_width), w_a, b_a, w_i, b_i, lam)


def kernel(x, meta_tokens, norm_gains, ret_w_in, ret_w_out, lru_w_in, lru_conv_w, lru_conv_b,
           lru_w_a, lru_b_a, lru_w_i, lru_b_i, lru_lambda, lru_w_out, ffn_w_in, ffn_conv_w,
           ffn_conv_b, ffn_w_out):
    meta = meta_tokens.astype(x.dtype)
    t = N_META + x.shape[1]

    pos = jnp.arange(t, dtype=F32)
    inv = ROPE_BASE ** (-jnp.arange(0, RET_QK_DIM, 2, dtype=F32) / RET_QK_DIM)
    ang = pos[:, None] * inv[None, :]
    cos, sin = jnp.cos(ang), jnp.sin(ang)

    ffn_w_in_bf = ffn_w_in.astype(BF16)
    ffn_w_out_bf = ffn_w_out.astype(BF16)

    g = norm_gains[0]
    p = _ret_in(x, meta, g[0], ret_w_in[0].astype(BF16), cos, sin)
    y = _retention(p)
    h = _out_proj_first(y, ret_w_out[0].astype(BF16), x, meta, g[1])
    h = _ffn(h, g[2:4], 0, ffn_w_in_bf, ffn_conv_w, ffn_conv_b, ffn_w_out_bf)

    g = norm_gains[1]
    y = _lru(h, g[0], lru_w_in[0].astype(BF16), lru_conv_w[0], lru_conv_b[0],
             lru_w_a[0].astype(BF16), lru_b_a[0], lru_w_i[0].astype(BF16), lru_b_i[0], lru_lambda[0])
    h = _out_proj(y, lru_w_out[0].astype(BF16), h, g[1])
    return _ffn(h, g[2:4], 1, ffn_w_in_bf, ffn_conv_w, ffn_conv_b, ffn_w_out_bf, drop_meta=True)
```

```python
import functools

import jax
import jax.numpy as jnp
from jax import lax
from jax.experimental import pallas as pl
from jax.experimental.pallas import tpu as pltpu

F32 = jnp.float32
BF16 = jnp.bfloat16

D_MODEL = 1024
N_META = 16
RMS_EPS = 1e-6

RET_HEADS = 4
RET_QK_DIM = D_MODEL // RET_HEADS
RET_V_DIM = 2 * D_MODEL // RET_HEADS
RET_CHUNK = 256
ROPE_BASE = 10000.0

LRU_BLOCKS = 4
LRU_BLOCK_W = D_MODEL // LRU_BLOCKS
LRU_C = 8.0
LRU_CONV_W = 4
LRU_CONV_LEFT = 2

FFN_HIDDEN = 3 * D_MODEL
FFN_CONV_W = 3
FFN_CONV_LEFT = 1

SUBLANES = 8
LANES = 128
BF16_ROWS = 16

VMEM_LIMIT_BYTES = 56 * 1024 * 1024


def _row_tile(t):
    for n in (2, 3, 4, 5, 6, 8):
        if t % n == 0 and (t // n) % BF16_ROWS == 0:
            return t // n
    return t


def _rms(x, g):
    ms = jnp.mean(x * x, axis=-1, keepdims=True)
    return x * lax.rsqrt(ms + RMS_EPS) * g


def _gelu_tanh(x):
    c = 0.7978845608028654
    return 0.5 * x * (1.0 + jnp.tanh(c * (x + 0.044715 * (x * x * x))))


def _params(sem):
    return pltpu.CompilerParams(dimension_semantics=sem, vmem_limit_bytes=VMEM_LIMIT_BYTES)


RET_IN_CHUNK = 2 * RET_HEADS * RET_QK_DIM
RET_IN_COLS = 512


def _ret_in_kernel(x_ref, meta_ref, g_ref, w_ref, cos_ref, sin_ref, p_ref, a_scr, *, rt, nrt):
    j = pl.program_id(1)
    n_meta = meta_ref.shape[0]
    seq = x_ref.shape[1]
    xt = _row_tile(seq)

    @pl.when(j == 0)
    def _():
        a_scr[0:n_meta, :] = _rms(meta_ref[...], g_ref[...]).astype(BF16)
        for r in range(seq // xt):
            a_scr[pl.ds(n_meta + r * xt, xt), :] = _rms(
                x_ref[0, pl.ds(r * xt, xt), :], g_ref[...]).astype(BF16)

    def proj(r, cb):
        cols = slice(cb * RET_IN_COLS, (cb + 1) * RET_IN_COLS)
        return jnp.dot(a_scr[pl.ds(r * rt, rt), :], w_ref[:, cols], preferred_element_type=F32)

    half = RET_QK_DIM // 2
    ncb = RET_IN_CHUNK // RET_IN_COLS

    @pl.when(j == 0)
    def _():
        for r in range(nrt):
            rows = pl.ds(r * rt, rt)
            c = cos_ref[rows, :]
            s = sin_ref[rows, :]
            for cb in range(ncb):
                p = proj(r, cb)
                is_k = cb * RET_IN_COLS >= RET_HEADS * RET_QK_DIM
                for hh in range(RET_IN_COLS // RET_QK_DIM):
                    lo = hh * RET_QK_DIM
                    x1 = p[:, lo:lo + half]
                    x2 = p[:, lo + half:lo + RET_QK_DIM]
                    y1 = x1 * c - x2 * s
                    y2 = x2 * c + x1 * s
                    if is_k:
                        y1 = y1 * RET_QK_DIM ** -0.5
                        y2 = y2 * RET_QK_DIM ** -0.5
                    out = cb * RET_IN_COLS + lo
                    p_ref[0, rows, out:out + half] = y1.astype(BF16)
                    p_ref[0, rows, out + half:out + RET_QK_DIM] = y2.astype(BF16)

    @pl.when(j != 0)
    def _():
        for r in range(nrt):
            for cb in range(ncb):
                cols = slice(cb * RET_IN_COLS, (cb + 1) * RET_IN_COLS)
                p_ref[0, pl.ds(r * rt, rt), cols] = proj(r, cb).astype(BF16)


def _ret_in(x, meta, gain, w_in, cos, sin):
    b, seq, d = x.shape
    n_meta = meta.shape[0]
    t = n_meta + seq
    n_out = w_in.shape[1]
    rt = _row_tile(t)
    nrt = t // rt
    half = RET_QK_DIM // 2
    return pl.pallas_call(
        functools.partial(_ret_in_kernel, rt=rt, nrt=nrt),
        grid=(b, n_out // RET_IN_CHUNK),
        in_specs=[
            pl.BlockSpec((1, seq, d), lambda i, j: (i, 0, 0)),
            pl.BlockSpec((n_meta, d), lambda i, j: (0, 0)),
            pl.BlockSpec((1, d), lambda i, j: (0, 0)),
            pl.BlockSpec((d, RET_IN_CHUNK), lambda i, j: (0, j)),
            pl.BlockSpec((t, half), lambda i, j: (0, 0)),
            pl.BlockSpec((t, half), lambda i, j: (0, 0)),
        ],
        out_specs=pl.BlockSpec((1, t, RET_IN_CHUNK), lambda i, j: (i, 0, j)),
        out_shape=jax.ShapeDtypeStruct((b, t, n_out), BF16),
        scratch_shapes=[pltpu.VMEM((t, d), BF16)],
        compiler_params=_params(("arbitrary", "arbitrary")),
        name="ret_in_proj",
    )(x, meta, gain.reshape(1, d), w_in, cos, sin)


RET_HEADS_PER_STEP = 2


def _retention_kernel(gc_ref, q_ref, k_ref, v_ref, g_ref, dm_ref, qf_ref, qb_ref, kf_ref, kb_ref,
                      y_ref, rf_scr, rb_scr, rbs_scr, *, n_full, lead):
    c = RET_CHUNK
    dk, dv = RET_QK_DIM, RET_V_DIM
    heads = range(RET_HEADS_PER_STEP)
    gcs = [gc_ref[pl.program_id(1) * RET_HEADS_PER_STEP + i] for i in heads]
    contract_rows = (((0,), (0,)), ((), ()))
    contract_feat = (((1,), (1,)), ((), ()))
    qcol = lambda i: slice(i * dk, (i + 1) * dk)
    vcol = lambda i: slice(i * dv, (i + 1) * dv)

    def state_update(r_scr, i, rows, decay, first):
        kd = k_ref[0, rows, qcol(i)] * decay
        kv = lax.dot_general(kd, v_ref[0, rows, vcol(i)], contract_rows, preferred_element_type=F32)
        r_scr[i] = kv if first else gcs[i] * r_scr[i] + kv

    all_t = pl.ds(0, c)
    lead_rows = pl.ds(0, lead)
    lead_t = pl.ds(c - lead, lead)
    full_rows = lambda n: pl.ds(lead + n * c, c)
    lead_slot = n_full - 1

    for n in reversed(range(n_full)):
        first = n == n_full - 1
        if not first:
            for i in heads:
                rbs_scr[i, n] = rb_scr[i].astype(BF16)
        for i in heads:
            state_update(rb_scr, i, full_rows(n), kb_ref[i, all_t, :], first)
    for i in heads:
        rbs_scr[i, lead_slot] = rb_scr[i].astype(BF16)

    def fwd_chunk(rows, trows, rb_slot, first, update):
        qs = [q_ref[0, rows, qcol(i)] for i in heads]
        ss = [lax.dot_general(qs[i], k_ref[0, rows, qcol(i)], contract_feat,
                              preferred_element_type=F32) for i in heads]
        ps = [(ss[i] * dm_ref[i, trows, trows]).astype(BF16) for i in heads]
        os_ = []
        for i in heads:
            o = jnp.dot(ps[i], v_ref[0, rows, vcol(i)], preferred_element_type=F32)
            if not first:
                o = o + jnp.dot(qs[i] * qf_ref[i, trows, :], rf_scr[i].astype(BF16),
                                preferred_element_type=F32)
            if rb_slot is not None:
                o = o + jnp.dot(qs[i] * qb_ref[i, trows, :], rbs_scr[i, rb_slot],
                                preferred_element_type=F32)
            os_.append(o)
        if update:
            for i in heads:
                state_update(rf_scr, i, rows, kf_ref[i, trows, :], first)
        for i in heads:
            o = os_[i]
            mu = jnp.mean(o, axis=-1, keepdims=True)
            oc = o - mu
            var = jnp.mean(oc * oc, axis=-1, keepdims=True)
            on = oc * lax.rsqrt(var + RMS_EPS)
            hg = 0.5 * g_ref[0, rows, vcol(i)]
            y_ref[0, rows, vcol(i)] = (hg * jnp.tanh(hg) + hg) * on.astype(BF16)

    fwd_chunk(lead_rows, lead_t, lead_slot, first=True, update=True)
    for n in range(n_full):
        last = n == n_full - 1
        fwd_chunk(full_rows(n), all_t, None if last else n, first=False, update=not last)


def _retention_tables():
    c = RET_CHUNK
    log_gamma = jnp.log1p(-jnp.exp2(-5.0 - jnp.arange(RET_HEADS, dtype=F32)))
    idx = jnp.arange(c, dtype=F32)
    lg = log_gamma[:, None, None]
    dmat = jnp.exp(lg * jnp.abs(idx[:, None] - idx[None, :])[None])
    qf = jnp.exp(log_gamma[:, None] * (idx + 1.0)[None])[..., None]
    qb = jnp.exp(log_gamma[:, None] * (c - idx)[None])[..., None]
    kf = jnp.exp(log_gamma[:, None] * (c - 1.0 - idx)[None])[..., None]
    kb = jnp.exp(log_gamma[:, None] * idx[None])[..., None]
    gc = jnp.exp(log_gamma * c)
    bk = lambda x: jnp.broadcast_to(x, (RET_HEADS, c, RET_QK_DIM)).astype(BF16)
    return gc, dmat, bk(qf), bk(qb), bk(kf), bk(kb)


def _retention(p):
    b, t, _ = p.shape
    c = RET_CHUNK
    lead = t % c
    n_full = t // c
    assert lead % BF16_ROWS == 0 and lead > 0
    gc, dmat, qf, qb, kf, kb = _retention_tables()
    dk, dv, nh = RET_QK_DIM, RET_V_DIM, RET_HEADS
    hp = RET_HEADS_PER_STEP
    ngrp = nh // hp
    k_off = nh * dk // (hp * dk)
    v_off = 2 * nh * dk // (hp * dv)
    g_off = v_off + ngrp
    tab = lambda w: pl.BlockSpec((hp, c, w), lambda i, h: (h, 0, 0))
    return pl.pallas_call(
        functools.partial(_retention_kernel, n_full=n_full, lead=lead),
        grid=(b, ngrp),
        in_specs=[
            pl.BlockSpec(memory_space=pltpu.SMEM),
            pl.BlockSpec((1, t, hp * dk), lambda i, h: (i, 0, h)),
            pl.BlockSpec((1, t, hp * dk), lambda i, h: (i, 0, k_off + h)),
            pl.BlockSpec((1, t, hp * dv), lambda i, h: (i, 0, v_off + h)),
            pl.BlockSpec((1, t, hp * dv), lambda i, h: (i, 0, g_off + h)),
            tab(c), tab(dk), tab(dk), tab(dk), tab(dk),
        ],
        out_specs=pl.BlockSpec((1, t, hp * dv), lambda i, h: (i, 0, h)),
        out_shape=jax.ShapeDtypeStruct((b, t, nh * dv), BF16),
        scratch_shapes=[
            pltpu.VMEM((hp, dk, dv), F32),
            pltpu.VMEM((hp, dk, dv), F32),
            pltpu.VMEM((hp, n_full, dk, dv), BF16),
        ],
        compiler_params=_params(("arbitrary", "arbitrary")),
        name="retention",
    )(gc, p, p, p, p, dmat, qf, qb, kf, kb)


def _row_pieces(rt):
    cut = -(-(rt // 2) // BF16_ROWS) * BF16_ROWS
    return ((0, cut), (cut, rt - cut))


def _out_proj_pieces(y_ref, w_ref, g_ref):
    pieces = _row_pieces(y_ref.shape[1])
    ms = [jnp.dot(y_ref[0, lo:lo + n, :], w_ref[...], preferred_element_type=F32)
          for lo, n in pieces]
    return [(lo, n, _rms(m, g_ref[...])) for (lo, n), m in zip(pieces, ms)]


def _out_proj_kernel(y_ref, w_ref, h_ref, g_ref, o_ref):
    for lo, n, res in _out_proj_pieces(y_ref, w_ref, g_ref):
        o_ref[0, lo:lo + n, :] = h_ref[0, lo:lo + n, :] + res


def _out_proj_first_kernel(y_ref, w_ref, x_ref, meta_ref, g_ref, o_ref):
    n_meta = meta_ref.shape[0]
    pieces = _out_proj_pieces(y_ref, w_ref, g_ref)

    @pl.when(pl.program_id(1) == 0)
    def _():
        for lo, n, res in pieces:
            if lo == 0:
                o_ref[0, 0:n_meta, :] = meta_ref[...] + res[0:n_meta, :]
                o_ref[0, n_meta:n, :] = x_ref[0, 0:n - n_meta, :] + res[n_meta:, :]
            else:
                o_ref[0, lo:lo + n, :] = x_ref[0, lo - n_meta:lo - n_meta + n, :] + res

    @pl.when(pl.program_id(1) != 0)
    def _():
        for lo, n, res in pieces:
            o_ref[0, lo:lo + n, :] = x_ref[0, lo:lo + n, :] + res


def _out_proj_first(y, w_out, x, meta, gain):
    b, seq, d = x.shape
    n_meta = meta.shape[0]
    t = n_meta + seq
    kdim = y.shape[-1]
    rt = _row_tile(t)
    return pl.pallas_call(
        _out_proj_first_kernel,
        grid=(b, t // rt),
        in_specs=[
            pl.BlockSpec((1, rt, kdim), lambda i, r: (i, r, 0)),
            pl.BlockSpec((kdim, d), lambda i, r: (0, 0)),
            pl.BlockSpec((pl.Element(1), pl.Element(rt), pl.Element(d)),
                         lambda i, r: (i, jnp.maximum(r * (rt // SUBLANES) - n_meta // SUBLANES, 0)
                                       * SUBLANES, 0)),
            pl.BlockSpec((n_meta, d), lambda i, r: (0, 0)),
            pl.BlockSpec((1, d), lambda i, r: (0, 0)),
        ],
        out_specs=pl.BlockSpec((1, rt, d), lambda i, r: (i, r, 0)),
        out_shape=jax.ShapeDtypeStruct((b, t, d), F32),
        compiler_params=_params(("arbitrary", "arbitrary")),
        name="out_proj_first",
    )(y, w_out, x, meta, gain.reshape(1, d))


def _out_proj(y, w_out, h, gain):
    b, t, d = h.shape
    kdim = y.shape[-1]
    rt = _row_tile(t)
    return pl.pallas_call(
        _out_proj_kernel,
        grid=(b, t // rt),
        in_specs=[
            pl.BlockSpec((1, rt, kdim), lambda i, r: (i, r, 0)),
            pl.BlockSpec((kdim, d), lambda i, r: (0, 0)),
            pl.BlockSpec((1, rt, d), lambda i, r: (i, r, 0)),
            pl.BlockSpec((1, d), lambda i, r: (0, 0)),
        ],
        out_specs=pl.BlockSpec((1, rt, d), lambda i, r: (i, r, 0)),
        out_shape=jax.ShapeDtypeStruct((b, t, d), F32),
        compiler_params=_params(("arbitrary", "arbitrary")),
        name="out_proj",
    )(y, w_out, h, gain.reshape(1, d))


FFN_CHUNK = 1024
FFN_SUB = 256
FFN_AHEAD = 2
FFN_HALO = BF16_ROWS


def _ffn_core(wv_ref, wg_ref, cwv_ref, cwg_ref, cbv_ref, cbg_ref, wo_ref, o_ref, a_scr, uv_scr,
              ug_scr, *, rt, fc):
    halo = FFN_HALO
    a = a_scr[...]
    spc = FFN_SUB // LANES
    nsub = fc // FFN_SUB

    def up_proj(c):
        cols = slice(c * FFN_SUB, (c + 1) * FFN_SUB)
        uv = jnp.dot(a, wv_ref[:, cols], preferred_element_type=F32)
        ug = jnp.dot(a, wg_ref[:, cols], preferred_element_type=F32)
        for s in range(spc):
            uv_scr[c * spc + s] = uv[:, s * LANES:(s + 1) * LANES]
            ug_scr[c * spc + s] = ug[:, s * LANES:(s + 1) * LANES]

    def conv(u_scr, c, cw_ref, cb_ref):
        outs = []
        for s in range(spc):
            lanes = slice(c * FFN_SUB + s * LANES, c * FFN_SUB + (s + 1) * LANES)
            acc = cb_ref[:, lanes]
            for k in range(FFN_CONV_W):
                lo = halo - FFN_CONV_LEFT + k
                acc = acc + cw_ref[k:k + 1, lanes] * u_scr[c * spc + s, lo:lo + rt, :]
            outs.append(acc)
        return jnp.concatenate(outs, axis=1)

    acc = None
    for c in range(FFN_AHEAD):
        up_proj(c)
    for c in range(nsub):
        if c + FFN_AHEAD < nsub:
            up_proj(c + FFN_AHEAD)
        val = conv(uv_scr, c, cwv_ref, cbv_ref)
        gate = conv(ug_scr, c, cwg_ref, cbg_ref)
        act = (_gelu_tanh(gate) * val).astype(BF16)
        part = jnp.dot(act, wo_ref[c * FFN_SUB:(c + 1) * FFN_SUB, :], preferred_element_type=F32)
        acc = part if acc is None else acc + part
    o_ref[0] += acc


def _ffn_kernel(hp_ref, h_ref, hn_ref, g_ref, wv_ref, wg_ref, cwv_ref, cwg_ref, cbv_ref, cbg_ref,
                wo_ref, o_ref, a_scr, uv_scr, ug_scr, *, rt, nrt, nj, fc):
    r = pl.program_id(1)
    j = pl.program_id(2)
    halo = FFN_HALO

    @pl.when(j == 0)
    def _():
        g_pre = g_ref[0:1, :]
        a_scr[halo:halo + rt, :] = _rms(h_ref[0], g_pre).astype(BF16)
        prev = jnp.where(r > 0, _rms(hp_ref[0], g_pre), 0.0)
        a_scr[0:halo, :] = prev.astype(BF16)
        nxt = jnp.where(r < nrt - 1, _rms(hn_ref[0], g_pre), 0.0)
        a_scr[halo + rt:halo + rt + halo, :] = nxt.astype(BF16)
        o_ref[...] = jnp.zeros_like(o_ref)

    _ffn_core(wv_ref, wg_ref, cwv_ref, cwg_ref, cbv_ref, cbg_ref, wo_ref, o_ref, a_scr, uv_scr,
              ug_scr, rt=rt, fc=fc)

    @pl.when(j == nj - 1)
    def _():
        o_ref[0] = h_ref[0] + _rms(o_ref[0], g_ref[1:2, :])


def _ffn_last_kernel(hw_ref, g_ref, wv_ref, wg_ref, cwv_ref, cwg_ref, cbv_ref, cbg_ref,
                     wo_ref, o_ref, a_scr, uv_scr, ug_scr, *, rt, nrt, nj, fc):
    r = pl.program_id(1)
    j = pl.program_id(2)
    halo = FFN_HALO
    is_last = r == nrt - 1
    g_pre = g_ref[0:1, :]
    g_post = g_ref[1:2, :]

    @pl.when(j == 0)
    def _():
        o_ref[...] = jnp.zeros_like(o_ref)

    @pl.when(jnp.logical_and(j == 0, jnp.logical_not(is_last)))
    def _():
        a_scr[...] = _rms(hw_ref[0], g_pre).astype(BF16)

    @pl.when(jnp.logical_and(j == 0, is_last))
    def _():
        a_scr[0:rt, :] = _rms(hw_ref[0, 2 * halo:, :], g_pre).astype(BF16)
        a_scr[rt:, :] = jnp.zeros((2 * halo, a_scr.shape[1]), BF16)

    _ffn_core(wv_ref, wg_ref, cwv_ref, cwg_ref, cbv_ref, cbg_ref, wo_ref, o_ref, a_scr, uv_scr,
              ug_scr, rt=rt, fc=fc)

    @pl.when(jnp.logical_and(j == nj - 1, jnp.logical_not(is_last)))
    def _():
        o_ref[0] = hw_ref[0, halo:halo + rt, :] + _rms(o_ref[0], g_post)

    @pl.when(jnp.logical_and(j == nj - 1, is_last))
    def _():
        o_ref[0, 0:rt - halo, :] = hw_ref[0, 3 * halo:, :] + _rms(o_ref[0, 0:rt - halo, :], g_post)


def _ffn(h, gains, layer, w_in, conv_w, conv_b, w_out, drop_meta=False):
    b, t, d = h.shape
    f = w_out.shape[1]
    rt = _row_tile(t)
    nrt = t // rt
    fc = FFN_CHUNK
    nj = f // fc
    halo = FFN_HALO
    hb = rt // halo
    last = t // halo - 1
    weight_specs = [
        pl.BlockSpec((2, d), lambda i, r, j: (0, 0)),
        pl.BlockSpec((None, d, fc), lambda i, r, j: (layer, 0, j)),
        pl.BlockSpec((None, d, fc), lambda i, r, j: (layer, 0, nj + j)),
        pl.BlockSpec((None, FFN_CONV_W, fc), lambda i, r, j: (layer, 0, j)),
        pl.BlockSpec((None, FFN_CONV_W, fc), lambda i, r, j: (layer, 0, nj + j)),
        pl.BlockSpec((None, 1, fc), lambda i, r, j: (layer, 0, j)),
        pl.BlockSpec((None, 1, fc), lambda i, r, j: (layer, 0, nj + j)),
        pl.BlockSpec((None, fc, d), lambda i, r, j: (layer, j, 0)),
    ]
    conv_b = conv_b.reshape(conv_b.shape[0], 1, 2 * f)
    weights = (gains, w_in, w_in, conv_w, conv_w, conv_b, conv_b, w_out)
    if drop_meta:
        assert halo == N_META
        win = rt + 2 * halo
        max_start = (t - win) // SUBLANES
        body = _ffn_last_kernel
        act_specs = [pl.BlockSpec(
            (pl.Element(1), pl.Element(win), pl.Element(d)),
            lambda i, r, j: (i, jnp.minimum(r * (rt // SUBLANES), max_start) * SUBLANES, 0))]
        acts = (h,)
        out_rows = t - halo
    else:
        body = _ffn_kernel
        act_specs = [
            pl.BlockSpec((1, halo, d), lambda i, r, j: (i, jnp.maximum(r * hb - 1, 0), 0)),
            pl.BlockSpec((1, rt, d), lambda i, r, j: (i, r, 0)),
            pl.BlockSpec((1, halo, d), lambda i, r, j: (i, jnp.minimum((r + 1) * hb, last), 0)),
        ]
        acts = (h, h, h)
        out_rows = t
    return pl.pallas_call(
        functools.partial(body, rt=rt, nrt=nrt, nj=nj, fc=fc),
        grid=(b, nrt, nj),
        in_specs=act_specs + weight_specs,
        out_specs=pl.BlockSpec((1, rt, d), lambda i, r, j: (i, r, 0)),
        out_shape=jax.ShapeDtypeStruct((b, out_rows, d), F32),
        scratch_shapes=[
            pltpu.VMEM((rt + 2 * halo, d), BF16),
            pltpu.VMEM((fc // LANES, rt + 2 * halo, LANES), F32),
            pltpu.VMEM((fc // LANES, rt + 2 * halo, LANES), F32),
        ],
        compiler_params=_params(("arbitrary", "arbitrary", "arbitrary")),
        name="conv_ffn_last" if drop_meta else "conv_ffn",
    )(*acts, *weights)


LRU_PAD = SUBLANES
LRU_SCAN_UNROLL = 6


def _lru_kernel(h_ref, g_ref, wgate_ref, wx_ref, cw_ref, cb_ref, wa_ref, ba_ref, wi_ref, bi_ref,
                lam_ref, y_ref, a_scr, xp_scr, xc_scr, gate_scr, av_scr, u_scr, hs_scr,
                *, rt, nrt, t):
    nb = pl.program_id(1)
    w = LRU_BLOCK_W
    nslab = w // LANES
    seg = t // SUBLANES
    pad = LRU_PAD
    lanes = lambda s: slice(s * LANES, (s + 1) * LANES)

    @pl.when(nb == 0)
    def _():
        for r in range(nrt):
            rows = pl.ds(r * rt, rt)
            a_scr[rows, :] = _rms(h_ref[0, rows, :], g_ref[...]).astype(BF16)

    for s in range(nslab):
        xp_scr[s, 0:pad, :] = jnp.zeros((pad, LANES), F32)
        xp_scr[s, pad + t:pad + t + pad, :] = jnp.zeros((pad, LANES), F32)

    def in_proj(r):
        ext = BF16_ROWS if r + 1 < nrt else 0
        xr = jnp.dot(a_scr[pl.ds(r * rt, rt + ext), :], wx_ref[...], preferred_element_type=F32)
        for s in range(nslab):
            xp_scr[s, pl.ds(pad + r * rt, rt + ext), :] = xr[:, lanes(s)]
        gate = jnp.dot(a_scr[pl.ds(r * rt, rt), :], wgate_ref[...], preferred_element_type=F32)
        gate_scr[pl.ds(r * rt, rt), :] = _gelu_tanh(gate).astype(BF16)

    def conv(r):
        outs = []
        for s in range(nslab):
            acc = cb_ref[:, lanes(s)]
            for k in range(LRU_CONV_W):
                lo = pad + r * rt - LRU_CONV_LEFT + k
                acc = acc + cw_ref[k:k + 1, lanes(s)] * xp_scr[s, lo:lo + rt, :]
            outs.append(acc)
        xc_scr[pl.ds(r * rt, rt), :] = jnp.concatenate(outs, axis=1)

    def gate_dots(r):
        hxb = (0.5 * xc_scr[pl.ds(r * rt, rt), :]).astype(BF16)
        return [(jnp.dot(hxb, wa_ref[d, 0], preferred_element_type=F32),
                 jnp.dot(hxb, wi_ref[d, 0], preferred_element_type=F32)) for d in range(2)]

    def gate_elementwise(r, raw):
        rows = pl.ds(r * rt, rt)
        x = xc_scr[rows, :]
        hx = 0.5 * x
        for d in range(2):
            ga_half, gi_half = raw[d]
            lam = lam_ref[d:d + 1, :]
            softplus_neg = jnp.maximum(-lam, 0.0) + jnp.log1p(jnp.exp(-jnp.abs(lam)))
            c1 = (-0.5 * LRU_C) * softplus_neg
            t_a = jnp.tanh(ga_half + 0.5 * ba_ref[d:d + 1, :])
            log_a = c1 * t_a + c1
            av = jnp.exp(log_a)
            z = jnp.tanh(log_a) * (-1.0 - av * av)
            mult = jnp.where(z > 0.0, z * lax.rsqrt(z), 0.0)
            t_i = jnp.tanh(gi_half + 0.5 * bi_ref[d:d + 1, :])
            uu = mult * (hx * t_i + hx)
            for s in range(nslab):
                av_scr[d, s, rows, :] = av[:, lanes(s)]
                u_scr[d, s, rows, :] = uu[:, lanes(s)]

    in_proj(0)
    for r in range(nrt):
        conv(r)
        raw = gate_dots(r)
        if r + 1 < nrt:
            in_proj(r + 1)
        gate_elementwise(r, raw)

    sub = lax.broadcasted_iota(jnp.int32, (SUBLANES, LANES), 0)
    chains = [(d, s) for d in range(2) for s in range(nslab)]

    def strided(d, ii):
        i = ii if d == 0 else seg - 1 - ii
        return pl.ds(i, SUBLANES, stride=seg)

    def pass1(ii, carry):
        out = []
        for (d, s), (hc, pc) in zip(chains, carry):
            a_i = av_scr[d, s, strided(d, ii), :]
            out.append((a_i * hc + u_scr[d, s, strided(d, ii), :], a_i * pc))
        return tuple(out)

    init = tuple((jnp.zeros((SUBLANES, LANES), F32), jnp.ones((SUBLANES, LANES), F32))
                 for _ in chains)
    ends = lax.fori_loop(0, seg, pass1, init, unroll=LRU_SCAN_UNROLL)

    cins = []
    for (d, s), (h_end, p_end) in zip(chains, ends):
        shift = 1 if d == 0 else SUBLANES - 1
        edge = 0 if d == 0 else SUBLANES - 1
        total = h_end
        for _ in range(SUBLANES - 1):
            moved = jnp.where(sub == edge, 0.0, pltpu.roll(total, shift, 0))
            total = h_end + p_end * moved
        cins.append(jnp.where(sub == edge, 0.0, pltpu.roll(total, shift, 0)))

    def pass2(ii, carry):
        out = []
        for (d, s), hc in zip(chains, carry):
            hc = av_scr[d, s, strided(d, ii), :] * hc + u_scr[d, s, strided(d, ii), :]
            hs_scr[d, s, strided(d, ii), :] = hc
            out.append(hc)
        return tuple(out)

    lax.fori_loop(0, seg, pass2, tuple(cins), unroll=LRU_SCAN_UNROLL)

    for r in range(nrt):
        rows = pl.ds(r * rt, rt)
        gl = gate_scr[rows, :].astype(F32)
        for s in range(nslab):
            y_ref[0, rows, lanes(s)] = (
                (hs_scr[0, s, rows, :] + hs_scr[1, s, rows, :]) * gl[:, lanes(s)]).astype(BF16)


def _lru(h, gain, w_in, conv_w, conv_b, w_a, b_a, w_i, b_i, lam):
    b, t, d = h.shape
    r_width = w_in.shape[1] // 2
    w = LRU_BLOCK_W
    nblk = r_width // w
    rt = _row_tile(t)
    nrt = t // rt
    assert t % SUBLANES == 0
    nslab = w // LANES
    vec = lambda rows: pl.BlockSpec((rows, w), lambda i, n: (0, n))
    return pl.pallas_call(
        functools.partial(_lru_kernel, rt=rt, nrt=nrt, t=t),
        grid=(b, nblk),
        in_specs=[
            pl.BlockSpec((1, t, d), lambda i, n: (i, 0, 0)),
            pl.BlockSpec((1, d), lambda i, n: (0, 0)),
            pl.BlockSpec((d, w), lambda i, n: (0, n)),
            pl.BlockSpec((d, w), lambda i, n: (0, nblk + n)),
            vec(LRU_CONV_W), vec(1),
            pl.BlockSpec((2, 1, w, w), lambda i, n: (0, n, 0, 0)), vec(2),
            pl.BlockSpec((2, 1, w, w), lambda i, n: (0, n, 0, 0)), vec(2),
            vec(2),
        ],
        out_specs=pl.BlockSpec((1, t, w), lambda i, n: (i, 0, n)),
        out_shape=jax.ShapeDtypeStruct((b, t, r_width), BF16),
        scratch_shapes=[
            pltpu.VMEM((t, d), BF16),
            pltpu.VMEM((nslab, t + 2 * LRU_PAD, LANES), F32),
            pltpu.VMEM((t, w), F32),
            pltpu.VMEM((t, w), BF16),
            pltpu.VMEM((2, nslab, t, LANES), F32),
            pltpu.VMEM((2, nslab, t, LANES), F32),
            pltpu.VMEM((2, nslab, t, LANES), F32),
        ],
        compiler_params=_params(("arbitrary", "arbitrary")),
        name="rglru",
    )(h, gain.reshape(1, d), w_in, w_in, conv_w, conv_b.reshape(1, r_width), w_a, b_a, w_i, b_i, lam)


def kernel(x, meta_tokens, norm_gains, ret_w_in, ret_w_out, lru_w_in, lru_conv_w, lru_conv_b,
           lru_w_a, lru_b_a, lru_w_i, lru_b_i, lru_lambda, lru_w_out, ffn_w_in, ffn_conv_w,
           ffn_conv_b, ffn_w_out):
    meta = meta_tokens.astype(x.dtype)
    t = N_META + x.shape[1]

    pos = jnp.arange(t, dtype=F32)
    inv = ROPE_BASE ** (-jnp.arange(0, RET_QK_DIM, 2, dtype=F32) / RET_QK_DIM)
    ang = pos[:, None] * inv[None, :]
    cos, sin = jnp.cos(ang), jnp.sin(ang)

    ffn_w_in_bf = ffn_w_in.astype(BF16)
    ffn_w_out_bf = ffn_w_out.astype(BF16)

    g = norm_gains[0]
    p = _ret_in(x, meta, g[0], ret_w_in[0].astype(BF16), cos, sin)
    y = _retention(p)
    h = _out_proj_first(y, ret_w_out[0].astype(BF16), x, meta, g[1])
    h = _ffn(h, g[2:4], 0, ffn_w_in_bf, ffn_conv_w, ffn_conv_b, ffn_w_out_bf)

    g = norm_gains[1]
    y = _lru(h, g[0], lru_w_in[0].astype(BF16), lru_conv_w[0], lru_conv_b[0],
             lru_w_a[0].astype(BF16), lru_b_a[0], lru_w_i[0].astype(BF16), lru_b_i[0], lru_lambda[0])
    h = _out_proj(y, lru_w_out[0].astype(BF16), h, g[1])
    return _ffn(h, g[2:4], 1, ffn_w_in_bf, ffn_conv_w, ffn_conv_b, ffn_w_out_bf, drop_meta=True)
```

```python
import functools

import jax
import jax.numpy as jnp
from jax import lax
from jax.experimental import pallas as pl
from jax.experimental.pallas import tpu as pltpu

F32 = jnp.float32
BF16 = jnp.bfloat16

D_MODEL = 1024
N_META = 16
RMS_EPS = 1e-6

RET_HEADS = 4
RET_QK_DIM = D_MODEL // RET_HEADS
RET_V_DIM = 2 * D_MODEL // RET_HEADS
RET_CHUNK = 256
ROPE_BASE = 10000.0

LRU_BLOCKS = 4
LRU_BLOCK_W = D_MODEL // LRU_BLOCKS
LRU_C = 8.0
LRU_CONV_W = 4
LRU_CONV_LEFT = 2

FFN_CONV_W = 3
FFN_CONV_LEFT = 1

SUBLANES = 8
LANES = 128
BF16_ROWS = 16

VMEM_LIMIT_BYTES = 56 * 1024 * 1024
OUT_PROJ_VMEM_BUDGET = 48 * 1024 * 1024


def _row_tile(t):
    for n in (2, 3, 4, 5, 6, 8):
        if t % n == 0 and (t // n) % BF16_ROWS == 0:
            return t // n
    return t


def _rms(x, g):
    ms = jnp.mean(x * x, axis=-1, keepdims=True)
    return x * lax.rsqrt(ms + RMS_EPS) * g


def _gelu_tanh(x):
    c = 0.7978845608028654
    return 0.5 * x * (1.0 + jnp.tanh(c * (x + 0.044715 * (x * x * x))))


def _params(sem):
    return pltpu.CompilerParams(dimension_semantics=sem, vmem_limit_bytes=VMEM_LIMIT_BYTES)


RET_IN_CHUNK = 2 * RET_HEADS * RET_QK_DIM
RET_IN_COLS = 512


def _ret_in_kernel(x_ref, meta_ref, g_ref, w_ref, cos_ref, sin_ref, p_ref, a_scr, *, rt, nrt):
    j = pl.program_id(1)
    n_meta = meta_ref.shape[0]
    seq = x_ref.shape[1]
    xt = _row_tile(seq)

    @pl.when(j == 0)
    def _():
        a_scr[0:n_meta, :] = _rms(meta_ref[...], g_ref[...]).astype(BF16)
        for r in range(seq // xt):
            a_scr[pl.ds(n_meta + r * xt, xt), :] = _rms(
                x_ref[0, pl.ds(r * xt, xt), :], g_ref[...]).astype(BF16)

    def proj(r, cb):
        cols = slice(cb * RET_IN_COLS, (cb + 1) * RET_IN_COLS)
        return jnp.dot(a_scr[pl.ds(r * rt, rt), :], w_ref[:, cols], preferred_element_type=F32)

    half = RET_QK_DIM // 2
    ncb = RET_IN_CHUNK // RET_IN_COLS

    @pl.when(j == 0)
    def _():
        for r in range(nrt):
            rows = pl.ds(r * rt, rt)
            c = cos_ref[rows, :]
            s = sin_ref[rows, :]
            for cb in range(ncb):
                p = proj(r, cb)
                is_k = cb * RET_IN_COLS >= RET_HEADS * RET_QK_DIM
                for hh in range(RET_IN_COLS // RET_QK_DIM):
                    lo = hh * RET_QK_DIM
                    x1 = p[:, lo:lo + half]
                    x2 = p[:, lo + half:lo + RET_QK_DIM]
                    y1 = x1 * c - x2 * s
                    y2 = x2 * c + x1 * s
                    if is_k:
                        y1 = y1 * RET_QK_DIM ** -0.5
                        y2 = y2 * RET_QK_DIM ** -0.5
                    out = cb * RET_IN_COLS + lo
                    p_ref[0, rows, out:out + half] = y1.astype(BF16)
                    p_ref[0, rows, out + half:out + RET_QK_DIM] = y2.astype(BF16)

    @pl.when(j != 0)
    def _():
        for r in range(nrt):
            for cb in range(ncb):
                cols = slice(cb * RET_IN_COLS, (cb + 1) * RET_IN_COLS)
                p_ref[0, pl.ds(r * rt, rt), cols] = proj(r, cb).astype(BF16)


def _ret_in(x, meta, gain, w_in, cos, sin):
    b, seq, d = x.shape
    n_meta = meta.shape[0]
    t = n_meta + seq
    n_out = w_in.shape[1]
    rt = _row_tile(t)
    nrt = t // rt
    half = RET_QK_DIM // 2
    return pl.pallas_call(
        functools.partial(_ret_in_kernel, rt=rt, nrt=nrt),
        grid=(b, n_out // RET_IN_CHUNK),
        in_specs=[
            pl.BlockSpec((1, seq, d), lambda i, j: (i, 0, 0)),
            pl.BlockSpec((n_meta, d), lambda i, j: (0, 0)),
            pl.BlockSpec((1, d), lambda i, j: (0, 0)),
            pl.BlockSpec((d, RET_IN_CHUNK), lambda i, j: (0, j)),
            pl.BlockSpec((t, half), lambda i, j: (0, 0)),
            pl.BlockSpec((t, half), lambda i, j: (0, 0)),
        ],
        out_specs=pl.BlockSpec((1, t, RET_IN_CHUNK), lambda i, j: (i, 0, j)),
        out_shape=jax.ShapeDtypeStruct((b, t, n_out), BF16),
        scratch_shapes=[pltpu.VMEM((t, d), BF16)],
        compiler_params=_params(("arbitrary", "arbitrary")),
        name="ret_in_proj",
    )(x, meta, gain.reshape(1, d), w_in, cos, sin)


RET_HEADS_PER_STEP = 2


def _retention_kernel(gc_ref, q_ref, k_ref, v_ref, g_ref, dm_ref, qf_ref, qb_ref, kf_ref, kb_ref,
                      y_ref, rf_scr, rb_scr, rbs_scr, *, n_full, lead):
    c = RET_CHUNK
    dk, dv = RET_QK_DIM, RET_V_DIM
    heads = range(RET_HEADS_PER_STEP)
    gcs = [gc_ref[pl.program_id(1) * RET_HEADS_PER_STEP + i] for i in heads]
    contract_rows = (((0,), (0,)), ((), ()))
    contract_feat = (((1,), (1,)), ((), ()))
    qcol = lambda i: slice(i * dk, (i + 1) * dk)
    vcol = lambda i: slice(i * dv, (i + 1) * dv)

    def state_update(r_scr, i, rows, decay, first):
        kd = k_ref[0, rows, qcol(i)] * decay
        kv = lax.dot_general(kd, v_ref[0, rows, vcol(i)], contract_rows, preferred_element_type=F32)
        r_scr[i] = kv if first else gcs[i] * r_scr[i] + kv

    all_t = pl.ds(0, c)
    lead_rows = pl.ds(0, lead)
    lead_t = pl.ds(c - lead, lead)
    full_rows = lambda n: pl.ds(lead + n * c, c)
    lead_slot = n_full - 1

    for n in reversed(range(n_full)):
        first = n == n_full - 1
        if not first:
            for i in heads:
                rbs_scr[i, n] = rb_scr[i].astype(BF16)
        for i in heads:
            state_update(rb_scr, i, full_rows(n), kb_ref[i, all_t, :], first)
    for i in heads:
        rbs_scr[i, lead_slot] = rb_scr[i].astype(BF16)

    def fwd_chunk(rows, trows, rb_slot, first, update):
        qs = [q_ref[0, rows, qcol(i)] for i in heads]
        ss = [lax.dot_general(qs[i], k_ref[0, rows, qcol(i)], contract_feat,
                              preferred_element_type=F32) for i in heads]
        ps = [(ss[i] * dm_ref[i, trows, trows]).astype(BF16) for i in heads]
        os_ = []
        for i in heads:
            o = jnp.dot(ps[i], v_ref[0, rows, vcol(i)], preferred_element_type=F32)
            if not first:
                o = o + jnp.dot(qs[i] * qf_ref[i, trows, :], rf_scr[i].astype(BF16),
                                preferred_element_type=F32)
            if rb_slot is not None:
                o = o + jnp.dot(qs[i] * qb_ref[i, trows, :], rbs_scr[i, rb_slot],
                                preferred_element_type=F32)
            os_.append(o)
        if update:
            for i in heads:
                state_update(rf_scr, i, rows, kf_ref[i, trows, :], first)
        for i in heads:
            o = os_[i]
            mu = jnp.mean(o, axis=-1, keepdims=True)
            oc = o - mu
            var = jnp.mean(oc * oc, axis=-1, keepdims=True)
            on = oc * lax.rsqrt(var + RMS_EPS)
            hg = 0.5 * g_ref[0, rows, vcol(i)]
            y_ref[0, rows, vcol(i)] = (hg * jnp.tanh(hg) + hg) * on.astype(BF16)

    fwd_chunk(lead_rows, lead_t, lead_slot, first=True, update=True)
    for n in range(n_full):
        last = n == n_full - 1
        fwd_chunk(full_rows(n), all_t, None if last else n, first=False, update=not last)


def _retention_tables():
    c = RET_CHUNK
    log_gamma = jnp.log1p(-jnp.exp2(-5.0 - jnp.arange(RET_HEADS, dtype=F32)))
    idx = jnp.arange(c, dtype=F32)
    lg = log_gamma[:, None, None]
    dmat = jnp.exp(lg * jnp.abs(idx[:, None] - idx[None, :])[None])
    qf = jnp.exp(log_gamma[:, None] * (idx + 1.0)[None])[..., None]
    qb = jnp.exp(log_gamma[:, None] * (c - idx)[None])[..., None]
    kf = jnp.exp(log_gamma[:, None] * (c - 1.0 - idx)[None])[..., None]
    kb = jnp.exp(log_gamma[:, None] * idx[None])[..., None]
    gc = jnp.exp(log_gamma * c)
    bk = lambda x: jnp.broadcast_to(x, (RET_HEADS, c, RET_QK_DIM)).astype(BF16)
    return gc, dmat, bk(qf), bk(qb), bk(kf), bk(kb)


def _retention(p):
    b, t, _ = p.shape
    c = RET_CHUNK
    lead = t % c
    n_full = t // c
    assert lead % BF16_ROWS == 0 and lead > 0
    gc, dmat, qf, qb, kf, kb = _retention_tables()
    dk, dv, nh = RET_QK_DIM, RET_V_DIM, RET_HEADS
    hp = RET_HEADS_PER_STEP
    ngrp = nh // hp
    k_off = nh * dk // (hp * dk)
    v_off = 2 * nh * dk // (hp * dv)
    g_off = v_off + ngrp
    tab = lambda w: pl.BlockSpec((hp, c, w), lambda i, h: (h, 0, 0))
    return pl.pallas_call(
        functools.partial(_retention_kernel, n_full=n_full, lead=lead),
        grid=(b, ngrp),
        in_specs=[
            pl.BlockSpec(memory_space=pltpu.SMEM),
            pl.BlockSpec((1, t, hp * dk), lambda i, h: (i, 0, h)),
            pl.BlockSpec((1, t, hp * dk), lambda i, h: (i, 0, k_off + h)),
            pl.BlockSpec((1, t, hp * dv), lambda i, h: (i, 0, v_off + h)),
            pl.BlockSpec((1, t, hp * dv), lambda i, h: (i, 0, g_off + h)),
            tab(c), tab(dk), tab(dk), tab(dk), tab(dk),
        ],
        out_specs=pl.BlockSpec((1, t, hp * dv), lambda i, h: (i, 0, h)),
        out_shape=jax.ShapeDtypeStruct((b, t, nh * dv), BF16),
        scratch_shapes=[
            pltpu.VMEM((hp, dk, dv), F32),
            pltpu.VMEM((hp, dk, dv), F32),
            pltpu.VMEM((hp, n_full, dk, dv), BF16),
        ],
        compiler_params=_params(("arbitrary", "arbitrary")),
        name="retention",
    )(gc, p, p, p, p, dmat, qf, qb, kf, kb)


def _out_proj_kernel(y_ref, w_ref, h_ref, g_ref, o_ref, *, rt):
    for r in range(y_ref.shape[1] // rt):
        rows = pl.ds(r * rt, rt)
        m = jnp.dot(y_ref[0, rows, :], w_ref[...], preferred_element_type=F32)
        o_ref[0, rows, :] = h_ref[0, rows, :] + _rms(m, g_ref[...])


def _out_proj_first_kernel(y_ref, w_ref, x_ref, meta_ref, g_ref, o_ref):
    n_meta = meta_ref.shape[0]
    rt = o_ref.shape[1]
    res = _rms(jnp.dot(y_ref[0], w_ref[...], preferred_element_type=F32), g_ref[...])

    @pl.when(pl.program_id(1) == 0)
    def _():
        o_ref[0, 0:n_meta, :] = meta_ref[...] + res[0:n_meta, :]
        o_ref[0, n_meta:, :] = x_ref[0, 0:rt - n_meta, :] + res[n_meta:, :]

    @pl.when(pl.program_id(1) != 0)
    def _():
        o_ref[0] = x_ref[0] + res


def _out_proj_first(y, w_out, x, meta, gain):
    b, seq, d = x.shape
    n_meta = meta.shape[0]
    t = n_meta + seq
    kdim = y.shape[-1]
    rt = _row_tile(t)
    return pl.pallas_call(
        _out_proj_first_kernel,
        grid=(b, t // rt),
        in_specs=[
            pl.BlockSpec((1, rt, kdim), lambda i, r: (i, r, 0)),
            pl.BlockSpec((kdim, d), lambda i, r: (0, 0)),
            pl.BlockSpec((pl.Element(1), pl.Element(rt), pl.Element(d)),
                         lambda i, r: (i, jnp.maximum(r * (rt // SUBLANES) - n_meta // SUBLANES, 0)
                                       * SUBLANES, 0)),
            pl.BlockSpec((n_meta, d), lambda i, r: (0, 0)),
            pl.BlockSpec((1, d), lambda i, r: (0, 0)),
        ],
        out_specs=pl.BlockSpec((1, rt, d), lambda i, r: (i, r, 0)),
        out_shape=jax.ShapeDtypeStruct((b, t, d), F32),
        compiler_params=_params(("arbitrary", "arbitrary")),
        name="out_proj_first",
    )(y, w_out, x, meta, gain.reshape(1, d))


def _out_proj(y, w_out, h, gain):
    b, t, d = h.shape
    kdim = y.shape[-1]
    rt = _row_tile(t)
    window_bytes = lambda rows: 2 * (rows * kdim * 2 + 2 * rows * d * 4) + 2 * kdim * d * 2
    blk = t if window_bytes(t) <= OUT_PROJ_VMEM_BUDGET else rt
    return pl.pallas_call(
        functools.partial(_out_proj_kernel, rt=rt),
        grid=(b, t // blk),
        in_specs=[
            pl.BlockSpec((1, blk, kdim), lambda i, r: (i, r, 0)),
            pl.BlockSpec((kdim, d), lambda i, r: (0, 0)),
            pl.BlockSpec((1, blk, d), lambda i, r: (i, r, 0)),
            pl.BlockSpec((1, d), lambda i, r: (0, 0)),
        ],
        out_specs=pl.BlockSpec((1, blk, d), lambda i, r: (i, r, 0)),
        out_shape=jax.ShapeDtypeStruct((b, t, d), F32),
        compiler_params=_params(("arbitrary", "arbitrary")),
        name="out_proj",
    )(y, w_out, h, gain.reshape(1, d))


FFN_CHUNK = 1024
FFN_SUB = 256
FFN_AHEAD = 2
FFN_HALO = BF16_ROWS


def _ffn_core(wv_ref, wg_ref, cwv_ref, cwg_ref, cbv_ref, cbg_ref, wo_ref, o_ref, a_scr, uv_scr,
              ug_scr, *, rt, fc):
    halo = FFN_HALO
    a = a_scr[...]
    spc = FFN_SUB // LANES
    nsub = fc // FFN_SUB

    def up_proj(c):
        cols = slice(c * FFN_SUB, (c + 1) * FFN_SUB)
        uv = jnp.dot(a, wv_ref[:, cols], preferred_element_type=F32)
        ug = jnp.dot(a, wg_ref[:, cols], preferred_element_type=F32)
        for s in range(spc):
            uv_scr[c * spc + s] = uv[:, s * LANES:(s + 1) * LANES]
            ug_scr[c * spc + s] = ug[:, s * LANES:(s + 1) * LANES]

    def conv(u_scr, c, cw_ref, cb_ref):
        outs = []
        for s in range(spc):
            lanes = slice(c * FFN_SUB + s * LANES, c * FFN_SUB + (s + 1) * LANES)
            acc = cb_ref[:, lanes]
            for k in range(FFN_CONV_W):
                lo = halo - FFN_CONV_LEFT + k
                acc = acc + cw_ref[k:k + 1, lanes] * u_scr[c * spc + s, lo:lo + rt, :]
            outs.append(acc)
        return jnp.concatenate(outs, axis=1)

    acc = None
    for c in range(FFN_AHEAD):
        up_proj(c)
    for c in range(nsub):
        if c + FFN_AHEAD < nsub:
            up_proj(c + FFN_AHEAD)
        val = conv(uv_scr, c, cwv_ref, cbv_ref)
        gate = conv(ug_scr, c, cwg_ref, cbg_ref)
        act = (_gelu_tanh(gate) * val).astype(BF16)
        part = jnp.dot(act, wo_ref[c * FFN_SUB:(c + 1) * FFN_SUB, :], preferred_element_type=F32)
        acc = part if acc is None else acc + part
    o_ref[0] += acc


def _ffn_kernel(hp_ref, h_ref, hn_ref, g_ref, wv_ref, wg_ref, cwv_ref, cwg_ref, cbv_ref, cbg_ref,
                wo_ref, o_ref, a_scr, uv_scr, ug_scr, *, rt, nrt, nj, fc):
    r = pl.program_id(1)
    j = pl.program_id(2)
    halo = FFN_HALO

    @pl.when(j == 0)
    def _():
        g_pre = g_ref[0:1, :]
        a_scr[halo:halo + rt, :] = _rms(h_ref[0], g_pre).astype(BF16)
        prev = jnp.where(r > 0, _rms(hp_ref[0], g_pre), 0.0)
        a_scr[0:halo, :] = prev.astype(BF16)
        nxt = jnp.where(r < nrt - 1, _rms(hn_ref[0], g_pre), 0.0)
        a_scr[halo + rt:halo + rt + halo, :] = nxt.astype(BF16)
        o_ref[...] = jnp.zeros_like(o_ref)

    _ffn_core(wv_ref, wg_ref, cwv_ref, cwg_ref, cbv_ref, cbg_ref, wo_ref, o_ref, a_scr, uv_scr,
              ug_scr, rt=rt, fc=fc)

    @pl.when(j == nj - 1)
    def _():
        o_ref[0] = h_ref[0] + _rms(o_ref[0], g_ref[1:2, :])


def _ffn_last_kernel(hw_ref, g_ref, wv_ref, wg_ref, cwv_ref, cwg_ref, cbv_ref, cbg_ref,
                     wo_ref, o_ref, a_scr, uv_scr, ug_scr, *, rt, nrt, nj, fc):
    r = pl.program_id(1)
    j = pl.program_id(2)
    halo = FFN_HALO
    is_last = r == nrt - 1
    g_pre = g_ref[0:1, :]
    g_post = g_ref[1:2, :]

    @pl.when(j == 0)
    def _():
        o_ref[...] = jnp.zeros_like(o_ref)

    @pl.when(jnp.logical_and(j == 0, jnp.logical_not(is_last)))
    def _():
        a_scr[...] = _rms(hw_ref[0], g_pre).astype(BF16)

    @pl.when(jnp.logical_and(j == 0, is_last))
    def _():
        a_scr[0:rt, :] = _rms(hw_ref[0, 2 * halo:, :], g_pre).astype(BF16)
        a_scr[rt:, :] = jnp.zeros((2 * halo, a_scr.shape[1]), BF16)

    _ffn_core(wv_ref, wg_ref, cwv_ref, cwg_ref, cbv_ref, cbg_ref, wo_ref, o_ref, a_scr, uv_scr,
              ug_scr, rt=rt, fc=fc)

    @pl.when(jnp.logical_and(j == nj - 1, jnp.logical_not(is_last)))
    def _():
        o_ref[0] = hw_ref[0, halo:halo + rt, :] + _rms(o_ref[0], g_post)

    @pl.when(jnp.logical_and(j == nj - 1, is_last))
    def _():
        o_ref[0, 0:rt - halo, :] = hw_ref[0, 3 * halo:, :] + _rms(o_ref[0, 0:rt - halo, :], g_post)


def _ffn(h, gains, layer, w_in, conv_w, conv_b, w_out, drop_meta=False):
    b, t, d = h.shape
    f = w_out.shape[1]
    rt = _row_tile(t)
    nrt = t // rt
    fc = FFN_CHUNK
    nj = f // fc
    halo = FFN_HALO
    hb = rt // halo
    last = t // halo - 1
    weight_specs = [
        pl.BlockSpec((2, d), lambda i, r, j: (0, 0)),
        pl.BlockSpec((None, d, fc), lambda i, r, j: (layer, 0, j)),
        pl.BlockSpec((None, d, fc), lambda i, r, j: (layer, 0, nj + j)),
        pl.BlockSpec((None, FFN_CONV_W, fc), lambda i, r, j: (layer, 0, j)),
        pl.BlockSpec((None, FFN_CONV_W, fc), lambda i, r, j: (layer, 0, nj + j)),
        pl.BlockSpec((None, 1, fc), lambda i, r, j: (layer, 0, j)),
        pl.BlockSpec((None, 1, fc), lambda i, r, j: (layer, 0, nj + j)),
        pl.BlockSpec((None, fc, d), lambda i, r, j: (layer, j, 0)),
    ]
    conv_b = conv_b.reshape(conv_b.shape[0], 1, 2 * f)
    weights = (gains, w_in, w_in, conv_w, conv_w, conv_b, conv_b, w_out)
    if drop_meta:
        assert halo == N_META
        win = rt + 2 * halo
        max_start = (t - win) // SUBLANES
        body = _ffn_last_kernel
        act_specs = [pl.BlockSpec(
            (pl.Element(1), pl.Element(win), pl.Element(d)),
            lambda i, r, j: (i, jnp.minimum(r * (rt // SUBLANES), max_start) * SUBLANES, 0))]
        acts = (h,)
        out_rows = t - halo
    else:
        body = _ffn_kernel
        act_specs = [
            pl.BlockSpec((1, halo, d), lambda i, r, j: (i, jnp.maximum(r * hb - 1, 0), 0)),
            pl.BlockSpec((1, rt, d), lambda i, r, j: (i, r, 0)),
            pl.BlockSpec((1, halo, d), lambda i, r, j: (i, jnp.minimum((r + 1) * hb, last), 0)),
        ]
        acts = (h, h, h)
        out_rows = t
    return pl.pallas_call(
        functools.partial(body, rt=rt, nrt=nrt, nj=nj, fc=fc),
        grid=(b, nrt, nj),
        in_specs=act_specs + weight_specs,
        out_specs=pl.BlockSpec((1, rt, d), lambda i, r, j: (i, r, 0)),
        out_shape=jax.ShapeDtypeStruct((b, out_rows, d), F32),
        scratch_shapes=[
            pltpu.VMEM((rt + 2 * halo, d), BF16),
            pltpu.VMEM((fc // LANES, rt + 2 * halo, LANES), F32),
            pltpu.VMEM((fc // LANES, rt + 2 * halo, LANES), F32),
        ],
        compiler_params=_params(("arbitrary", "arbitrary", "arbitrary")),
        name="conv_ffn_last" if drop_meta else "conv_ffn",
    )(*acts, *weights)


LRU_PAD = SUBLANES
LRU_SCAN_UNROLL = 6


def _lru_kernel(h_ref, g_ref, wgate_ref, wx_ref, cw_ref, cb_ref, wa_ref, ba_ref, wi_ref, bi_ref,
                lam_ref, y_ref, a_scr, xp_scr, xc_scr, gate_scr, av_scr, u_scr, hs_scr,
                *, rt, nrt, t):
    nb = pl.program_id(1)
    w = LRU_BLOCK_W
    nslab = w // LANES
    seg = t // SUBLANES
    pad = LRU_PAD
    lanes = lambda s: slice(s * LANES, (s + 1) * LANES)

    @pl.when(nb == 0)
    def _():
        for r in range(nrt):
            rows = pl.ds(r * rt, rt)
            a_scr[rows, :] = _rms(h_ref[0, rows, :], g_ref[...]).astype(BF16)

    for s in range(nslab):
        xp_scr[s, 0:pad, :] = jnp.zeros((pad, LANES), F32)
        xp_scr[s, pad + t:pad + t + pad, :] = jnp.zeros((pad, LANES), F32)

    def in_proj(r):
        ext = BF16_ROWS if r + 1 < nrt else 0
        xr = jnp.dot(a_scr[pl.ds(r * rt, rt + ext), :], wx_ref[...], preferred_element_type=F32)
        for s in range(nslab):
            xp_scr[s, pl.ds(pad + r * rt, rt + ext), :] = xr[:, lanes(s)]
        gate = jnp.dot(a_scr[pl.ds(r * rt, rt), :], wgate_ref[...], preferred_element_type=F32)
        gate_scr[pl.ds(r * rt, rt), :] = _gelu_tanh(gate).astype(BF16)

    def conv(r):
        outs = []
        for s in range(nslab):
            acc = cb_ref[:, lanes(s)]
            for k in range(LRU_CONV_W):
                lo = pad + r * rt - LRU_CONV_LEFT + k
                acc = acc + cw_ref[k:k + 1, lanes(s)] * xp_scr[s, lo:lo + rt, :]
            outs.append(acc)
        xc_scr[pl.ds(r * rt, rt), :] = jnp.concatenate(outs, axis=1)

    def gate_dots(r):
        hxb = (0.5 * xc_scr[pl.ds(r * rt, rt), :]).astype(BF16)
        return [(jnp.dot(hxb, wa_ref[d, 0], preferred_element_type=F32),
                 jnp.dot(hxb, wi_ref[d, 0], preferred_element_type=F32)) for d in range(2)]

    def gate_elementwise(r, raw):
        rows = pl.ds(r * rt, rt)
        x = xc_scr[rows, :]
        hx = 0.5 * x
        for d in range(2):
            ga_half, gi_half = raw[d]
            lam = lam_ref[d:d + 1, :]
            softplus_neg = jnp.maximum(-lam, 0.0) + jnp.log1p(jnp.exp(-jnp.abs(lam)))
            c1 = (-0.5 * LRU_C) * softplus_neg
            t_a = jnp.tanh(ga_half + 0.5 * ba_ref[d:d + 1, :])
            log_a = c1 * t_a + c1
            av = jnp.exp(log_a)
            z = jnp.tanh(log_a) * (-1.0 - av * av)
            mult = jnp.where(z > 0.0, z * lax.rsqrt(z), 0.0)
            t_i = jnp.tanh(gi_half + 0.5 * bi_ref[d:d + 1, :])
            uu = mult * (hx * t_i + hx)
            for s in range(nslab):
                av_scr[d, s, rows, :] = av[:, lanes(s)]
                u_scr[d, s, rows, :] = uu[:, lanes(s)]

    in_proj(0)
    for r in range(nrt):
        conv(r)
        raw = gate_dots(r)
        if r + 1 < nrt:
            in_proj(r + 1)
        gate_elementwise(r, raw)

    sub = lax.broadcasted_iota(jnp.int32, (SUBLANES, LANES), 0)
    chains = [(d, s) for d in range(2) for s in range(nslab)]

    def strided(d, ii):
        i = ii if d == 0 else seg - 1 - ii
        return pl.ds(i, SUBLANES, stride=seg)

    def pass1(ii, carry):
        out = []
        for (d, s), (hc, pc) in zip(chains, carry):
            a_i = av_scr[d, s, strided(d, ii), :]
            out.append((a_i * hc + u_scr[d, s, strided(d, ii), :], a_i * pc))
        return tuple(out)

    init = tuple((jnp.zeros((SUBLANES, LANES), F32), jnp.ones((SUBLANES, LANES), F32))
                 for _ in chains)
    ends = lax.fori_loop(0, seg, pass1, init, unroll=LRU_SCAN_UNROLL)

    cins = []
    for (d, s), (h_end, p_end) in zip(chains, ends):
        shift = 1 if d == 0 else SUBLANES - 1
        edge = 0 if d == 0 else SUBLANES - 1
        total = h_end
        for _ in range(SUBLANES - 1):
            moved = jnp.where(sub == edge, 0.0, pltpu.roll(total, shift, 0))
            total = h_end + p_end * moved
        cins.append(jnp.where(sub == edge, 0.0, pltpu.roll(total, shift, 0)))

    def pass2(ii, carry):
        out = []
        for (d, s), hc in zip(chains, carry):
            hc = av_scr[d, s, strided(d, ii), :] * hc + u_scr[d, s, strided(d, ii), :]
            hs_scr[d, s, strided(d, ii), :] = hc
            out.append(hc)
        return tuple(out)

    lax.fori_loop(0, seg, pass2, tuple(cins), unroll=LRU_SCAN_UNROLL)

    for r in range(nrt):
        rows = pl.ds(r * rt, rt)
        gl = gate_scr[rows, :].astype(F32)
        for s in range(nslab):
            y_ref[0, rows, lanes(s)] = (
                (hs_scr[0, s, rows, :] + hs_scr[1, s, rows, :]) * gl[:, lanes(s)]).astype(BF16)


def _lru(h, gain, w_in, conv_w, conv_b, w_a, b_a, w_i, b_i, lam):
    b, t, d = h.shape
    r_width = w_in.shape[1] // 2
    w = LRU_BLOCK_W
    nblk = r_width // w
    rt = _row_tile(t)
    nrt = t // rt
    assert t % SUBLANES == 0
    nslab = w // LANES
    vec = lambda rows: pl.BlockSpec((rows, w), lambda i, n: (0, n))
    return pl.pallas_call(
        functools.partial(_lru_kernel, rt=rt, nrt=nrt, t=t),
        grid=(b, nblk),
        in_specs=[
            pl.BlockSpec((1, t, d), lambda i, n: (i, 0, 0)),
            pl.BlockSpec((1, d), lambda i, n: (0, 0)),
            pl.BlockSpec((d, w), lambda i, n: (0, n)),
            pl.BlockSpec((d, w), lambda i, n: (0, nblk + n)),
            vec(LRU_CONV_W), vec(1),
            pl.BlockSpec((2, 1, w, w), lambda i, n: (0, n, 0, 0)), vec(2),
            pl.BlockSpec((2, 1, w, w), lambda i, n: (0, n, 0, 0)), vec(2),
            vec(2),
        ],
        out_specs=pl.BlockSpec((1, t, w), lambda i, n: (i, 0, n)),
        out_shape=jax.ShapeDtypeStruct((b, t, r_width), BF16),
        scratch_shapes=[
            pltpu.VMEM((t, d), BF16),
            pltpu.VMEM((nslab, t + 2 * LRU_PAD, LANES), F32),
            pltpu.VMEM((t, w), F32),
            pltpu.VMEM((t, w), BF16),
            pltpu.VMEM((2, nslab, t, LANES), F32),
            pltpu.VMEM((2, nslab, t, LANES), F32),
            pltpu.VMEM((2, nslab, t, LANES), F32),
        ],
        compiler_params=_params(("arbitrary", "arbitrary")),
        name="rglru",
    )(h, gain.reshape(1, d), w_in, w_in, conv_w, conv_b.reshape(1, r_width), w_a, b_a, w_i, b_i, lam)


def kernel(x, meta_tokens, norm_gains, ret_w_in, ret_w_out, lru_w_in, lru_conv_w, lru_conv_b,
           lru_w_a, lru_b_a, lru_w_i, lru_b_i, lru_lambda, lru_w_out, ffn_w_in, ffn_conv_w,
           ffn_conv_b, ffn_w_out):
    meta = meta_tokens.astype(x.dtype)
    t = N_META + x.shape[1]

    pos = jnp.arange(t, dtype=F32)
    inv = ROPE_BASE ** (-jnp.arange(0, RET_QK_DIM, 2, dtype=F32) / RET_QK_DIM)
    ang = pos[:, None] * inv[None, :]
    cos, sin = jnp.cos(ang), jnp.sin(ang)

    ffn_w_in_bf = ffn_w_in.astype(BF16)
    ffn_w_out_bf = ffn_w_out.astype(BF16)

    g = norm_gains[0]
    p = _ret_in(x, meta, g[0], ret_w_in[0].astype(BF16), cos, sin)
    y = _retention(p)
    h = _out_proj_first(y, ret_w_out[0].astype(BF16), x, meta, g[1])
    h = _ffn(h, g[2:4], 0, ffn_w_in_bf, ffn_conv_w, ffn_conv_b, ffn_w_out_bf)

    g = norm_gains[1]
    y = _lru(h, g[0], lru_w_in[0].astype(BF16), lru_conv_w[0], lru_conv_b[0],
             lru_w_a[0].astype(BF16), lru_b_a[0], lru_w_i[0].astype(BF16), lru_b_i[0], lru_lambda[0])
    h = _out_proj(y, lru_w_out[0].astype(BF16), h, g[1])
    return _ffn(h, g[2:4], 1, ffn_w_in_bf, ffn_conv_w, ffn_conv_b, ffn_w_out_bf, drop_meta=True)
```

```python
import functools

import jax
import jax.numpy as jnp
from jax import lax
from jax.experimental import pallas as pl
from jax.experimental.pallas import tpu as pltpu

F32 = jnp.float32
BF16 = jnp.bfloat16

D_MODEL = 1024
N_META = 16
RMS_EPS = 1e-6

RET_HEADS = 4
RET_QK_DIM = D_MODEL // RET_HEADS
RET_V_DIM = 2 * D_MODEL // RET_HEADS
RET_CHUNK = 256
ROPE_BASE = 10000.0

LRU_BLOCKS = 4
LRU_BLOCK_W = D_MODEL // LRU_BLOCKS
LRU_C = 8.0
LRU_CONV_W = 4
LRU_CONV_LEFT = 2

FFN_CONV_W = 3
FFN_CONV_LEFT = 1

SUBLANES = 8
LANES = 128
BF16_ROWS = 16

VMEM_LIMIT_BYTES = 56 * 1024 * 1024
OUT_PROJ_VMEM_BUDGET = 48 * 1024 * 1024


def _row_tile(t):
    for n in (2, 3, 4, 5, 6, 8):
        if t % n == 0 and (t // n) % BF16_ROWS == 0:
            return t // n
    return t


def _rms(x, g):
    ms = jnp.mean(x * x, axis=-1, keepdims=True)
    return x * lax.rsqrt(ms + RMS_EPS) * g


def _gelu_tanh(x):
    c = 0.7978845608028654
    return 0.5 * x * (1.0 + jnp.tanh(c * (x + 0.044715 * (x * x * x))))


def _params(sem):
    return pltpu.CompilerParams(dimension_semantics=sem, vmem_limit_bytes=VMEM_LIMIT_BYTES)


RET_IN_CHUNK = 2 * RET_HEADS * RET_QK_DIM
RET_IN_COLS = 512


def _ret_in_kernel(x_ref, meta_ref, g_ref, w_ref, cos_ref, sin_ref, p_ref, a_scr, *, rt, nrt):
    j = pl.program_id(1)
    n_meta = meta_ref.shape[0]
    seq = x_ref.shape[1]
    xt = _row_tile(seq)

    @pl.when(j == 0)
    def _():
        a_scr[0:n_meta, :] = _rms(meta_ref[...], g_ref[...]).astype(BF16)
        for r in range(seq // xt):
            a_scr[pl.ds(n_meta + r * xt, xt), :] = _rms(
                x_ref[0, pl.ds(r * xt, xt), :], g_ref[...]).astype(BF16)

    def proj(r, cb):
        cols = slice(cb * RET_IN_COLS, (cb + 1) * RET_IN_COLS)
        return jnp.dot(a_scr[pl.ds(r * rt, rt), :], w_ref[:, cols], preferred_element_type=F32)

    half = RET_QK_DIM // 2
    ncb = RET_IN_CHUNK // RET_IN_COLS

    @pl.when(j == 0)
    def _():
        for r in range(nrt):
            rows = pl.ds(r * rt, rt)
            c = cos_ref[rows, :]
            s = sin_ref[rows, :]
            for cb in range(ncb):
                p = proj(r, cb)
                is_k = cb * RET_IN_COLS >= RET_HEADS * RET_QK_DIM
                for hh in range(RET_IN_COLS // RET_QK_DIM):
                    lo = hh * RET_QK_DIM
                    x1 = p[:, lo:lo + half]
                    x2 = p[:, lo + half:lo + RET_QK_DIM]
                    y1 = x1 * c - x2 * s
                    y2 = x2 * c + x1 * s
                    if is_k:
                        y1 = y1 * RET_QK_DIM ** -0.5
                        y2 = y2 * RET_QK_DIM ** -0.5
                    out = cb * RET_IN_COLS + lo
                    p_ref[0, rows, out:out + half] = y1.astype(BF16)
                    p_ref[0, rows, out + half:out + RET_QK_DIM] = y2.astype(BF16)

    @pl.when(j != 0)
    def _():
        for r in range(nrt):
            for cb in range(ncb):
                cols = slice(cb * RET_IN_COLS, (cb + 1) * RET_IN_COLS)
                p_ref[0, pl.ds(r * rt, rt), cols] = proj(r, cb).astype(BF16)


def _ret_in(x, meta, gain, w_in, cos, sin):
    b, seq, d = x.shape
    n_meta = meta.shape[0]
    t = n_meta + seq
    n_out = w_in.shape[1]
    rt = _row_tile(t)
    nrt = t // rt
    half = RET_QK_DIM // 2
    return pl.pallas_call(
        functools.partial(_ret_in_kernel, rt=rt, nrt=nrt),
        grid=(b, n_out // RET_IN_CHUNK),
        in_specs=[
            pl.BlockSpec((1, seq, d), lambda i, j: (i, 0, 0)),
            pl.BlockSpec((n_meta, d), lambda i, j: (0, 0)),
            pl.BlockSpec((1, d), lambda i, j: (0, 0)),
            pl.BlockSpec((d, RET_IN_CHUNK), lambda i, j: (0, j)),
            pl.BlockSpec((t, half), lambda i, j: (0, 0)),
            pl.BlockSpec((t, half), lambda i, j: (0, 0)),
        ],
        out_specs=pl.BlockSpec((1, t, RET_IN_CHUNK), lambda i, j: (i, 0, j)),
        out_shape=jax.ShapeDtypeStruct((b, t, n_out), BF16),
        scratch_shapes=[pltpu.VMEM((t, d), BF16)],
        compiler_params=_params(("arbitrary", "arbitrary")),
        name="ret_in_proj",
    )(x, meta, gain.reshape(1, d), w_in, cos, sin)


RET_HEADS_PER_STEP = 2


def _retention_kernel(gc_ref, q_ref, k_ref, v_ref, g_ref, dm_ref, qf_ref, qb_ref, kf_ref, kb_ref,
                      y_ref, rf_scr, rb_scr, rbs_scr, *, n_full, lead):
    c = RET_CHUNK
    dk, dv = RET_QK_DIM, RET_V_DIM
    heads = range(RET_HEADS_PER_STEP)
    gcs = [gc_ref[pl.program_id(1) * RET_HEADS_PER_STEP + i] for i in heads]
    contract_rows = (((0,), (0,)), ((), ()))
    contract_feat = (((1,), (1,)), ((), ()))
    qcol = lambda i: slice(i * dk, (i + 1) * dk)
    vcol = lambda i: slice(i * dv, (i + 1) * dv)

    def state_update(r_scr, i, rows, decay, first):
        kd = k_ref[0, rows, qcol(i)] * decay
        kv = lax.dot_general(kd, v_ref[0, rows, vcol(i)], contract_rows, preferred_element_type=F32)
        r_scr[i] = kv if first else gcs[i] * r_scr[i] + kv

    all_t = pl.ds(0, c)
    lead_rows = pl.ds(0, lead)
    lead_t = pl.ds(c - lead, lead)
    full_rows = lambda n: pl.ds(lead + n * c, c)
    lead_slot = n_full - 1

    for n in reversed(range(n_full)):
        first = n == n_full - 1
        if not first:
            for i in heads:
                rbs_scr[i, n] = rb_scr[i].astype(BF16)
        for i in heads:
            state_update(rb_scr, i, full_rows(n), kb_ref[i, all_t, :], first)
    for i in heads:
        rbs_scr[i, lead_slot] = rb_scr[i].astype(BF16)

    def fwd_chunk(rows, trows, rb_slot, first, update):
        qs = [q_ref[0, rows, qcol(i)] for i in heads]
        ss = [lax.dot_general(qs[i], k_ref[0, rows, qcol(i)], contract_feat,
                              preferred_element_type=F32) for i in heads]
        ps = [(ss[i] * dm_ref[i, trows, trows]).astype(BF16) for i in heads]
        os_ = []
        for i in heads:
            o = jnp.dot(ps[i], v_ref[0, rows, vcol(i)], preferred_element_type=F32)
            if not first:
                o = o + jnp.dot(qs[i] * qf_ref[i, trows, :], rf_scr[i].astype(BF16),
                                preferred_element_type=F32)
            if rb_slot is not None:
                o = o + jnp.dot(qs[i] * qb_ref[i, trows, :], rbs_scr[i, rb_slot],
                                preferred_element_type=F32)
            os_.append(o)
        if update:
            for i in heads:
                state_update(rf_scr, i, rows, kf_ref[i, trows, :], first)
        for i in heads:
            o = os_[i]
            mu = jnp.mean(o, axis=-1, keepdims=True)
            oc = o - mu
            var = jnp.mean(oc * oc, axis=-1, keepdims=True)
            on = oc * lax.rsqrt(var + RMS_EPS)
            hg = 0.5 * g_ref[0, rows, vcol(i)]
            y_ref[0, rows, vcol(i)] = (hg * jnp.tanh(hg) + hg) * on.astype(BF16)

    fwd_chunk(lead_rows, lead_t, lead_slot, first=True, update=True)
    for n in range(n_full):
        last = n == n_full - 1
        fwd_chunk(full_rows(n), all_t, None if last else n, first=False, update=not last)


def _retention_tables():
    c = RET_CHUNK
    log_gamma = jnp.log1p(-jnp.exp2(-5.0 - jnp.arange(RET_HEADS, dtype=F32)))
    idx = jnp.arange(c, dtype=F32)
    lg = log_gamma[:, None, None]
    dmat = jnp.exp(lg * jnp.abs(idx[:, None] - idx[None, :])[None])
    qf = jnp.exp(log_gamma[:, None] * (idx + 1.0)[None])[..., None]
    qb = jnp.exp(log_gamma[:, None] * (c - idx)[None])[..., None]
    kf = jnp.exp(log_gamma[:, None] * (c - 1.0 - idx)[None])[..., None]
    kb = jnp.exp(log_gamma[:, None] * idx[None])[..., None]
    gc = jnp.exp(log_gamma * c)
    bk = lambda x: jnp.broadcast_to(x, (RET_HEADS, c, RET_QK_DIM)).astype(BF16)
    return gc, dmat, bk(qf), bk(qb), bk(kf), bk(kb)


def _retention(p):
    b, t, _ = p.shape
    c = RET_CHUNK
    lead = t % c
    n_full = t // c
    assert lead % BF16_ROWS == 0 and lead > 0
    gc, dmat, qf, qb, kf, kb = _retention_tables()
    dk, dv, nh = RET_QK_DIM, RET_V_DIM, RET_HEADS
    hp = RET_HEADS_PER_STEP
    ngrp = nh // hp
    k_off = nh * dk // (hp * dk)
    v_off = 2 * nh * dk // (hp * dv)
    g_off = v_off + ngrp
    tab = lambda w: pl.BlockSpec((hp, c, w), lambda i, h: (h, 0, 0))
    return pl.pallas_call(
        functools.partial(_retention_kernel, n_full=n_full, lead=lead),
        grid=(b, ngrp),
        in_specs=[
            pl.BlockSpec(memory_space=pltpu.SMEM),
            pl.BlockSpec((1, t, hp * dk), lambda i, h: (i, 0, h)),
            pl.BlockSpec((1, t, hp * dk), lambda i, h: (i, 0, k_off + h)),
            pl.BlockSpec((1, t, hp * dv), lambda i, h: (i, 0, v_off + h)),
            pl.BlockSpec((1, t, hp * dv), lambda i, h: (i, 0, g_off + h)),
            tab(c), tab(dk), tab(dk), tab(dk), tab(dk),
        ],
        out_specs=pl.BlockSpec((1, t, hp * dv), lambda i, h: (i, 0, h)),
        out_shape=jax.ShapeDtypeStruct((b, t, nh * dv), BF16),
        scratch_shapes=[
            pltpu.VMEM((hp, dk, dv), F32),
            pltpu.VMEM((hp, dk, dv), F32),
            pltpu.VMEM((hp, n_full, dk, dv), BF16),
        ],
        compiler_params=_params(("arbitrary", "arbitrary")),
        name="retention",
    )(gc, p, p, p, p, dmat, qf, qb, kf, kb)


def _out_proj_kernel(y_ref, w_ref, h_ref, g_ref, o_ref, *, rt):
    for r in range(y_ref.shape[1] // rt):
        rows = pl.ds(r * rt, rt)
        m = jnp.dot(y_ref[0, rows, :], w_ref[...], preferred_element_type=F32)
        o_ref[0, rows, :] = h_ref[0, rows, :] + _rms(m, g_ref[...])


def _out_proj_first_kernel(y_ref, w_ref, x_ref, meta_ref, g_ref, o_ref):
    n_meta = meta_ref.shape[0]
    rt = o_ref.shape[1]
    res = _rms(jnp.dot(y_ref[0], w_ref[...], preferred_element_type=F32), g_ref[...])

    @pl.when(pl.program_id(1) == 0)
    def _():
        o_ref[0, 0:n_meta, :] = meta_ref[...] + res[0:n_meta, :]
        o_ref[0, n_meta:, :] = x_ref[0, 0:rt - n_meta, :] + res[n_meta:, :]

    @pl.when(pl.program_id(1) != 0)
    def _():
        o_ref[0] = x_ref[0] + res


def _out_proj_first(y, w_out, x, meta, gain):
    b, seq, d = x.shape
    n_meta = meta.shape[0]
    t = n_meta + seq
    kdim = y.shape[-1]
    rt = _row_tile(t)
    return pl.pallas_call(
        _out_proj_first_kernel,
        grid=(b, t // rt),
        in_specs=[
            pl.BlockSpec((1, rt, kdim), lambda i, r: (i, r, 0)),
            pl.BlockSpec((kdim, d), lambda i, r: (0, 0)),
            pl.BlockSpec((pl.Element(1), pl.Element(rt), pl.Element(d)),
                         lambda i, r: (i, jnp.maximum(r * (rt // SUBLANES) - n_meta // SUBLANES, 0)
                                       * SUBLANES, 0)),
            pl.BlockSpec((n_meta, d), lambda i, r: (0, 0)),
            pl.BlockSpec((1, d), lambda i, r: (0, 0)),
        ],
        out_specs=pl.BlockSpec((1, rt, d), lambda i, r: (i, r, 0)),
        out_shape=jax.ShapeDtypeStruct((b, t, d), F32),
        compiler_params=_params(("arbitrary", "arbitrary")),
        name="out_proj_first",
    )(y, w_out, x, meta, gain.reshape(1, d))


def _out_proj(y, w_out, h, gain):
    b, t, d = h.shape
    kdim = y.shape[-1]
    rt = _row_tile(t)
    window_bytes = lambda rows: 2 * (rows * kdim * 2 + 2 * rows * d * 4) + 2 * kdim * d * 2
    blk = t if window_bytes(t) <= OUT_PROJ_VMEM_BUDGET else rt
    return pl.pallas_call(
        functools.partial(_out_proj_kernel, rt=rt),
        grid=(b, t // blk),
        in_specs=[
            pl.BlockSpec((1, blk, kdim), lambda i, r: (i, r, 0)),
            pl.BlockSpec((kdim, d), lambda i, r: (0, 0)),
            pl.BlockSpec((1, blk, d), lambda i, r: (i, r, 0)),
            pl.BlockSpec((1, d), lambda i, r: (0, 0)),
        ],
        out_specs=pl.BlockSpec((1, blk, d), lambda i, r: (i, r, 0)),
        out_shape=jax.ShapeDtypeStruct((b, t, d), F32),
        compiler_params=_params(("arbitrary", "arbitrary")),
        name="out_proj",
    )(y, w_out, h, gain.reshape(1, d))


FFN_CHUNK = 1024
FFN_SUB = 256
FFN_AHEAD = 2
FFN_HALO = BF16_ROWS


def _ffn_core(wv_ref, wg_ref, cwv_ref, cwg_ref, cbv_ref, cbg_ref, wo_ref, o_ref, a_scr, uv_scr,
              ug_scr, *, rt, fc):
    halo = FFN_HALO
    a = a_scr[...]
    spc = FFN_SUB // LANES
    nsub = fc // FFN_SUB

    def up_proj(c):
        cols = slice(c * FFN_SUB, (c + 1) * FFN_SUB)
        uv = jnp.dot(a, wv_ref[:, cols], preferred_element_type=F32)
        ug = jnp.dot(a, wg_ref[:, cols], preferred_element_type=F32)
        for s in range(spc):
            uv_scr[c * spc + s] = uv[:, s * LANES:(s + 1) * LANES]
            ug_scr[c * spc + s] = ug[:, s * LANES:(s + 1) * LANES]

    def conv(u_scr, c, cw_ref, cb_ref):
        outs = []
        for s in range(spc):
            lanes = slice(c * FFN_SUB + s * LANES, c * FFN_SUB + (s + 1) * LANES)
            acc = cb_ref[:, lanes]
            for k in range(FFN_CONV_W):
                lo = halo - FFN_CONV_LEFT + k
                acc = acc + cw_ref[k:k + 1, lanes] * u_scr[c * spc + s, lo:lo + rt, :]
            outs.append(acc)
        return jnp.concatenate(outs, axis=1)

    acc = None
    for c in range(FFN_AHEAD):
        up_proj(c)
    for c in range(nsub):
        if c + FFN_AHEAD < nsub:
            up_proj(c + FFN_AHEAD)
        val = conv(uv_scr, c, cwv_ref, cbv_ref)
        gate = conv(ug_scr, c, cwg_ref, cbg_ref)
        act = (_gelu_tanh(gate) * val).astype(BF16)
        part = jnp.dot(act, wo_ref[c * FFN_SUB:(c + 1) * FFN_SUB, :], preferred_element_type=F32)
        acc = part if acc is None else acc + part
    o_ref[0] += acc


def _ffn_chunks(w_ref, cw_ref, cb_ref, wo_ref, o_ref, a_scr, uv_scr, ug_scr, *, rt, nj, fc):
    o_ref[...] = jnp.zeros_like(o_ref)

    def chunk(j, carry):
        _ffn_core(w_ref.at[j], w_ref.at[nj + j], cw_ref.at[j], cw_ref.at[nj + j], cb_ref.at[j],
                  cb_ref.at[nj + j], wo_ref.at[j], o_ref, a_scr, uv_scr, ug_scr, rt=rt, fc=fc)
        return carry

    lax.fori_loop(0, nj, chunk, 0)


def _ffn_kernel(hp_ref, h_ref, hn_ref, g_ref, w_ref, cw_ref, cb_ref, wo_ref, o_ref,
                a_scr, uv_scr, ug_scr, *, rt, nrt, nj, fc):
    r = pl.program_id(1)
    halo = FFN_HALO
    g_pre = g_ref[0:1, :]
    a_scr[halo:halo + rt, :] = _rms(h_ref[0], g_pre).astype(BF16)
    prev = jnp.where(r > 0, _rms(hp_ref[0], g_pre), 0.0)
    a_scr[0:halo, :] = prev.astype(BF16)
    nxt = jnp.where(r < nrt - 1, _rms(hn_ref[0], g_pre), 0.0)
    a_scr[halo + rt:halo + rt + halo, :] = nxt.astype(BF16)

    _ffn_chunks(w_ref, cw_ref, cb_ref, wo_ref, o_ref, a_scr, uv_scr, ug_scr, rt=rt, nj=nj, fc=fc)

    o_ref[0] = h_ref[0] + _rms(o_ref[0], g_ref[1:2, :])


def _ffn_last_kernel(hw_ref, g_ref, w_ref, cw_ref, cb_ref, wo_ref, o_ref,
                     a_scr, uv_scr, ug_scr, *, rt, nrt, nj, fc):
    halo = FFN_HALO
    is_last = pl.program_id(1) == nrt - 1
    g_pre = g_ref[0:1, :]
    g_post = g_ref[1:2, :]

    @pl.when(jnp.logical_not(is_last))
    def _():
        a_scr[...] = _rms(hw_ref[0], g_pre).astype(BF16)

    @pl.when(is_last)
    def _():
        a_scr[0:rt, :] = _rms(hw_ref[0, 2 * halo:, :], g_pre).astype(BF16)
        a_scr[rt:, :] = jnp.zeros((2 * halo, a_scr.shape[1]), BF16)

    _ffn_chunks(w_ref, cw_ref, cb_ref, wo_ref, o_ref, a_scr, uv_scr, ug_scr, rt=rt, nj=nj, fc=fc)

    @pl.when(jnp.logical_not(is_last))
    def _():
        o_ref[0] = hw_ref[0, halo:halo + rt, :] + _rms(o_ref[0], g_post)

    @pl.when(is_last)
    def _():
        o_ref[0, 0:rt - halo, :] = hw_ref[0, 3 * halo:, :] + _rms(o_ref[0, 0:rt - halo, :], g_post)


def _ffn_weights(w_in, conv_w, conv_b, w_out):
    nl, d, f2 = w_in.shape
    fc = FFN_CHUNK
    by_chunk = lambda x: x.reshape(nl, x.shape[1], f2 // fc, fc).transpose(0, 2, 1, 3)
    return (by_chunk(w_in).astype(BF16), by_chunk(conv_w), by_chunk(conv_b[:, None, :]),
            w_out.reshape(nl, f2 // 2 // fc, fc, d).astype(BF16))


def _ffn(h, gains, layer, weights, drop_meta=False):
    b, t, d = h.shape
    wo = weights[-1]
    nj, fc = wo.shape[1], wo.shape[2]
    rt = _row_tile(t)
    nrt = t // rt
    halo = FFN_HALO
    hb = rt // halo
    last = t // halo - 1
    resident = lambda x: pl.BlockSpec((None,) + x.shape[1:], lambda i, r: (layer, 0, 0, 0),
                                      pipeline_mode=pl.Buffered(1))
    weight_specs = [pl.BlockSpec((2, d), lambda i, r: (0, 0))] + [resident(x) for x in weights]
    if drop_meta:
        assert halo == N_META
        win = rt + 2 * halo
        max_start = (t - win) // SUBLANES
        body = _ffn_last_kernel
        act_specs = [pl.BlockSpec(
            (pl.Element(1), pl.Element(win), pl.Element(d)),
            lambda i, r: (i, jnp.minimum(r * (rt // SUBLANES), max_start) * SUBLANES, 0))]
        acts = (h,)
        out_rows = t - halo
    else:
        body = _ffn_kernel
        act_specs = [
            pl.BlockSpec((1, halo, d), lambda i, r: (i, jnp.maximum(r * hb - 1, 0), 0)),
            pl.BlockSpec((1, rt, d), lambda i, r: (i, r, 0)),
            pl.BlockSpec((1, halo, d), lambda i, r: (i, jnp.minimum((r + 1) * hb, last), 0)),
        ]
        acts = (h, h, h)
        out_rows = t
    return pl.pallas_call(
        functools.partial(body, rt=rt, nrt=nrt, nj=nj, fc=fc),
        grid=(b, nrt),
        in_specs=act_specs + weight_specs,
        out_specs=pl.BlockSpec((1, rt, d), lambda i, r: (i, r, 0)),
        out_shape=jax.ShapeDtypeStruct((b, out_rows, d), F32),
        scratch_shapes=[
            pltpu.VMEM((rt + 2 * halo, d), BF16),
            pltpu.VMEM((fc // LANES, rt + 2 * halo, LANES), F32),
            pltpu.VMEM((fc // LANES, rt + 2 * halo, LANES), F32),
        ],
        compiler_params=_params(("arbitrary", "arbitrary")),
        name="conv_ffn_last" if drop_meta else "conv_ffn",
    )(*acts, gains, *weights)


LRU_PAD = SUBLANES
LRU_SCAN_UNROLL = 6


def _lru_kernel(h_ref, g_ref, wgate_ref, wx_ref, cw_ref, cb_ref, wa_ref, ba_ref, wi_ref, bi_ref,
                lam_ref, y_ref, a_scr, xp_scr, xc_scr, gate_scr, av_scr, u_scr, hs_scr,
                *, rt, nrt, t):
    nb = pl.program_id(1)
    w = LRU_BLOCK_W
    nslab = w // LANES
    seg = t // SUBLANES
    pad = LRU_PAD
    lanes = lambda s: slice(s * LANES, (s + 1) * LANES)

    @pl.when(nb == 0)
    def _():
        for r in range(nrt):
            rows = pl.ds(r * rt, rt)
            a_scr[rows, :] = _rms(h_ref[0, rows, :], g_ref[...]).astype(BF16)

    for s in range(nslab):
        xp_scr[s, 0:pad, :] = jnp.zeros((pad, LANES), F32)
        xp_scr[s, pad + t:pad + t + pad, :] = jnp.zeros((pad, LANES), F32)

    def in_proj(r):
        ext = BF16_ROWS if r + 1 < nrt else 0
        xr = jnp.dot(a_scr[pl.ds(r * rt, rt + ext), :], wx_ref[...], preferred_element_type=F32)
        for s in range(nslab):
            xp_scr[s, pl.ds(pad + r * rt, rt + ext), :] = xr[:, lanes(s)]
        gate = jnp.dot(a_scr[pl.ds(r * rt, rt), :], wgate_ref[...], preferred_element_type=F32)
        gate_scr[pl.ds(r * rt, rt), :] = _gelu_tanh(gate).astype(BF16)

    def conv(r):
        outs = []
        for s in range(nslab):
            acc = cb_ref[:, lanes(s)]
            for k in range(LRU_CONV_W):
                lo = pad + r * rt - LRU_CONV_LEFT + k
                acc = acc + cw_ref[k:k + 1, lanes(s)] * xp_scr[s, lo:lo + rt, :]
            outs.append(acc)
        xc_scr[pl.ds(r * rt, rt), :] = jnp.concatenate(outs, axis=1)

    def gate_dots(r):
        hxb = (0.5 * xc_scr[pl.ds(r * rt, rt), :]).astype(BF16)
        return [(jnp.dot(hxb, wa_ref[d, 0], preferred_element_type=F32),
                 jnp.dot(hxb, wi_ref[d, 0], preferred_element_type=F32)) for d in range(2)]

    def gate_elementwise(r, raw):
        rows = pl.ds(r * rt, rt)
        x = xc_scr[rows, :]
        hx = 0.5 * x
        for d in range(2):
            ga_half, gi_half = raw[d]
            lam = lam_ref[d:d + 1, :]
            softplus_neg = jnp.maximum(-lam, 0.0) + jnp.log1p(jnp.exp(-jnp.abs(lam)))
            c1 = (-0.5 * LRU_C) * softplus_neg
            t_a = jnp.tanh(ga_half + 0.5 * ba_ref[d:d + 1, :])
            log_a = c1 * t_a + c1
            av = jnp.exp(log_a)
            z = jnp.tanh(log_a) * (-1.0 - av * av)
            mult = jnp.where(z > 0.0, z * lax.rsqrt(z), 0.0)
            t_i = jnp.tanh(gi_half + 0.5 * bi_ref[d:d + 1, :])
            uu = mult * (hx * t_i + hx)
            for s in range(nslab):
                av_scr[d, s, rows, :] = av[:, lanes(s)]
                u_scr[d, s, rows, :] = uu[:, lanes(s)]

    in_proj(0)
    for r in range(nrt):
        conv(r)
        raw = gate_dots(r)
        if r + 1 < nrt:
            in_proj(r + 1)
        gate_elementwise(r, raw)

    sub = lax.broadcasted_iota(jnp.int32, (SUBLANES, LANES), 0)
    chains = [(d, s) for d in range(2) for s in range(nslab)]

    def strided(d, ii):
        i = ii if d == 0 else seg - 1 - ii
        return pl.ds(i, SUBLANES, stride=seg)

    def pass1(ii, carry):
        out = []
        for (d, s), (hc, pc) in zip(chains, carry):
            a_i = av_scr[d, s, strided(d, ii), :]
            out.append((a_i * hc + u_scr[d, s, strided(d, ii), :], a_i * pc))
        return tuple(out)

    init = tuple((jnp.zeros((SUBLANES, LANES), F32), jnp.ones((SUBLANES, LANES), F32))
                 for _ in chains)
    ends = lax.fori_loop(0, seg, pass1, init, unroll=LRU_SCAN_UNROLL)

    cins = []
    for (d, s), (h_end, p_end) in zip(chains, ends):
        shift = 1 if d == 0 else SUBLANES - 1
        edge = 0 if d == 0 else SUBLANES - 1
        total = h_end
        for _ in range(SUBLANES - 1):
            moved = jnp.where(sub == edge, 0.0, pltpu.roll(total, shift, 0))
            total = h_end + p_end * moved
        cins.append(jnp.where(sub == edge, 0.0, pltpu.roll(total, shift, 0)))

    def pass2(ii, carry):
        out = []
        for (d, s), hc in zip(chains, carry):
            hc = av_scr[d, s, strided(d, ii), :] * hc + u_scr[d, s, strided(d, ii), :]
            hs_scr[d, s, strided(d, ii), :] = hc
            out.append(hc)
        return tuple(out)

    lax.fori_loop(0, seg, pass2, tuple(cins), unroll=LRU_SCAN_UNROLL)

    for r in range(nrt):
        rows = pl.ds(r * rt, rt)
        gl = gate_scr[rows, :].astype(F32)
        for s in range(nslab):
            y_ref[0, rows, lanes(s)] = (
                (hs_scr[0, s, rows, :] + hs_scr[1, s, rows, :]) * gl[:, lanes(s)]).astype(BF16)


def _lru(h, gain, w_in, conv_w, conv_b, w_a, b_a, w_i, b_i, lam):
    b, t, d = h.shape
    r_width = w_in.shape[1] // 2
    w = LRU_BLOCK_W
    nblk = r_width // w
    rt = _row_tile(t)
    nrt = t // rt
    assert t % SUBLANES == 0
    nslab = w // LANES
    vec = lambda rows: pl.BlockSpec((rows, w), lambda i, n: (0, n))
    return pl.pallas_call(
        functools.partial(_lru_kernel, rt=rt, nrt=nrt, t=t),
        grid=(b, nblk),
        in_specs=[
            pl.BlockSpec((1, t, d), lambda i, n: (i, 0, 0)),
            pl.BlockSpec((1, d), lambda i, n: (0, 0)),
            pl.BlockSpec((d, w), lambda i, n: (0, n)),
            pl.BlockSpec((d, w), lambda i, n: (0, nblk + n)),
            vec(LRU_CONV_W), vec(1),
            pl.BlockSpec((2, 1, w, w), lambda i, n: (0, n, 0, 0)), vec(2),
            pl.BlockSpec((2, 1, w, w), lambda i, n: (0, n, 0, 0)), vec(2),
            vec(2),
        ],
        out_specs=pl.BlockSpec((1, t, w), lambda i, n: (i, 0, n)),
        out_shape=jax.ShapeDtypeStruct((b, t, r_width), BF16),
        scratch_shapes=[
            pltpu.VMEM((t, d), BF16),
            pltpu.VMEM((nslab, t + 2 * LRU_PAD, LANES), F32),
            pltpu.VMEM((t, w), F32),
            pltpu.VMEM((t, w), BF16),
            pltpu.VMEM((2, nslab, t, LANES), F32),
            pltpu.VMEM((2, nslab, t, LANES), F32),
            pltpu.VMEM((2, nslab, t, LANES), F32),
        ],
        compiler_params=_params(("arbitrary", "arbitrary")),
        name="rglru",
    )(h, gain.reshape(1, d), w_in, w_in, conv_w, conv_b.reshape(1, r_width), w_a, b_a, w_i, b_i, lam)


def kernel(x, meta_tokens, norm_gains, ret_w_in, ret_w_out, lru_w_in, lru_conv_w, lru_conv_b,
           lru_w_a, lru_b_a, lru_w_i, lru_b_i, lru_lambda, lru_w_out, ffn_w_in, ffn_conv_w,
           ffn_conv_b, ffn_w_out):
    meta = meta_tokens.astype(x.dtype)
    t = N_META + x.shape[1]

    pos = jnp.arange(t, dtype=F32)
    inv = ROPE_BASE ** (-jnp.arange(0, RET_QK_DIM, 2, dtype=F32) / RET_QK_DIM)
    ang = pos[:, None] * inv[None, :]
    cos, sin = jnp.cos(ang), jnp.sin(ang)

    ffn_weights = _ffn_weights(ffn_w_in, ffn_conv_w, ffn_conv_b, ffn_w_out)

    g = norm_gains[0]
    p = _ret_in(x, meta, g[0], ret_w_in[0].astype(BF16), cos, sin)
    y = _retention(p)
    h = _out_proj_first(y, ret_w_out[0].astype(BF16), x, meta, g[1])
    h = _ffn(h, g[2:4], 0, ffn_weights)

    g = norm_gains[1]
    y = _lru(h, g[0], lru_w_in[0].astype(BF16), lru_conv_w[0], lru_conv_b[0],
             lru_w_a[0].astype(BF16), lru_b_a[0], lru_w_i[0].astype(BF16), lru_b_i[0], lru_lambda[0])
    h = _out_proj(y, lru_w_out[0].astype(BF16), h, g[1])
    return _ffn(h, g[2:4], 1, ffn_weights, drop_meta=True)
```

```python
import functools

import jax
import jax.numpy as jnp
from jax import lax
from jax.experimental import pallas as pl
from jax.experimental.pallas import tpu as pltpu

F32 = jnp.float32
BF16 = jnp.bfloat16

D_MODEL = 1024
N_META = 16
RMS_EPS = 1e-6

RET_HEADS = 4
RET_QK_DIM = D_MODEL // RET_HEADS
RET_V_DIM = 2 * D_MODEL // RET_HEADS
RET_CHUNK = 256
ROPE_BASE = 10000.0

LRU_BLOCKS = 4
LRU_BLOCK_W = D_MODEL // LRU_BLOCKS
LRU_C = 8.0
LRU_CONV_W = 4
LRU_CONV_LEFT = 2

FFN_CONV_W = 3
FFN_CONV_LEFT = 1

SUBLANES = 8
LANES = 128
BF16_ROWS = 16

VMEM_LIMIT_BYTES = 56 * 1024 * 1024
OUT_PROJ_VMEM_BUDGET = 48 * 1024 * 1024


def _row_tile(t):
    for n in (2, 3, 4, 5, 6, 8):
        if t % n == 0 and (t // n) % BF16_ROWS == 0:
            return t // n
    return t


def _rms(x, g):
    ms = jnp.mean(x * x, axis=-1, keepdims=True)
    return x * lax.rsqrt(ms + RMS_EPS) * g


def _gelu_tanh(x):
    c = 0.7978845608028654
    inner = x * (c + (c * 0.044715) * (x * x))
    return (0.5 * x) * (1.0 + jnp.tanh(inner))


def _params(sem):
    return pltpu.CompilerParams(dimension_semantics=sem, vmem_limit_bytes=VMEM_LIMIT_BYTES)


RET_IN_CHUNK = 2 * RET_HEADS * RET_QK_DIM
RET_IN_COLS = 512


def _ret_in_kernel(x_ref, meta_ref, g_ref, w_ref, cos_ref, sin_ref, p_ref, a_scr, *, rt, nrt):
    j = pl.program_id(1)
    n_meta = meta_ref.shape[0]
    seq = x_ref.shape[1]
    xt = _row_tile(seq)

    @pl.when(j == 0)
    def _():
        a_scr[0:n_meta, :] = _rms(meta_ref[...], g_ref[...]).astype(BF16)
        for r in range(seq // xt):
            a_scr[pl.ds(n_meta + r * xt, xt), :] = _rms(
                x_ref[0, pl.ds(r * xt, xt), :], g_ref[...]).astype(BF16)

    def proj(r, cb):
        cols = slice(cb * RET_IN_COLS, (cb + 1) * RET_IN_COLS)
        return jnp.dot(a_scr[pl.ds(r * rt, rt), :], w_ref[:, cols], preferred_element_type=F32)

    half = RET_QK_DIM // 2
    ncb = RET_IN_CHUNK // RET_IN_COLS

    @pl.when(j == 0)
    def _():
        for r in range(nrt):
            rows = pl.ds(r * rt, rt)
            c = cos_ref[rows, :]
            s = sin_ref[rows, :]
            for cb in range(ncb):
                p = proj(r, cb)
                is_k = cb * RET_IN_COLS >= RET_HEADS * RET_QK_DIM
                for hh in range(RET_IN_COLS // RET_QK_DIM):
                    lo = hh * RET_QK_DIM
                    x1 = p[:, lo:lo + half]
                    x2 = p[:, lo + half:lo + RET_QK_DIM]
                    y1 = x1 * c - x2 * s
                    y2 = x2 * c + x1 * s
                    if is_k:
                        y1 = y1 * RET_QK_DIM ** -0.5
                        y2 = y2 * RET_QK_DIM ** -0.5
                    out = cb * RET_IN_COLS + lo
                    p_ref[0, rows, out:out + half] = y1.astype(BF16)
                    p_ref[0, rows, out + half:out + RET_QK_DIM] = y2.astype(BF16)

    @pl.when(j != 0)
    def _():
        for r in range(nrt):
            for cb in range(ncb):
                cols = slice(cb * RET_IN_COLS, (cb + 1) * RET_IN_COLS)
                p_ref[0, pl.ds(r * rt, rt), cols] = proj(r, cb).astype(BF16)


def _ret_in(x, meta, gain, w_in, cos, sin):
    b, seq, d = x.shape
    n_meta = meta.shape[0]
    t = n_meta + seq
    n_out = w_in.shape[1]
    rt = _row_tile(t)
    nrt = t // rt
    half = RET_QK_DIM // 2
    return pl.pallas_call(
        functools.partial(_ret_in_kernel, rt=rt, nrt=nrt),
        grid=(b, n_out // RET_IN_CHUNK),
        in_specs=[
            pl.BlockSpec((1, seq, d), lambda i, j: (i, 0, 0)),
            pl.BlockSpec((n_meta, d), lambda i, j: (0, 0)),
            pl.BlockSpec((1, d), lambda i, j: (0, 0)),
            pl.BlockSpec((d, RET_IN_CHUNK), lambda i, j: (0, j)),
            pl.BlockSpec((t, half), lambda i, j: (0, 0)),
            pl.BlockSpec((t, half), lambda i, j: (0, 0)),
        ],
        out_specs=pl.BlockSpec((1, t, RET_IN_CHUNK), lambda i, j: (i, 0, j)),
        out_shape=jax.ShapeDtypeStruct((b, t, n_out), BF16),
        scratch_shapes=[pltpu.VMEM((t, d), BF16)],
        compiler_params=_params(("arbitrary", "arbitrary")),
        name="ret_in_proj",
    )(x, meta, gain.reshape(1, d), w_in, cos, sin)


RET_HEADS_PER_STEP = 2


def _retention_kernel(gc_ref, q_ref, k_ref, v_ref, g_ref, dm_ref, qf_ref, qb_ref, kf_ref, kb_ref,
                      y_ref, rf_scr, rb_scr, rbs_scr, *, n_full, lead):
    c = RET_CHUNK
    dk, dv = RET_QK_DIM, RET_V_DIM
    heads = range(RET_HEADS_PER_STEP)
    gcs = [gc_ref[pl.program_id(1) * RET_HEADS_PER_STEP + i] for i in heads]
    contract_rows = (((0,), (0,)), ((), ()))
    contract_feat = (((1,), (1,)), ((), ()))
    qcol = lambda i: slice(i * dk, (i + 1) * dk)
    vcol = lambda i: slice(i * dv, (i + 1) * dv)

    def state_update(r_scr, i, rows, decay, first):
        kd = k_ref[0, rows, qcol(i)] * decay
        kv = lax.dot_general(kd, v_ref[0, rows, vcol(i)], contract_rows, preferred_element_type=F32)
        r_scr[i] = kv if first else gcs[i] * r_scr[i] + kv

    all_t = pl.ds(0, c)
    lead_rows = pl.ds(0, lead)
    lead_t = pl.ds(c - lead, lead)
    full_rows = lambda n: pl.ds(lead + n * c, c)
    lead_slot = n_full - 1

    for n in reversed(range(n_full)):
        first = n == n_full - 1
        if not first:
            for i in heads:
                rbs_scr[i, n] = rb_scr[i].astype(BF16)
        for i in heads:
            state_update(rb_scr, i, full_rows(n), kb_ref[i, all_t, :], first)
    for i in heads:
        rbs_scr[i, lead_slot] = rb_scr[i].astype(BF16)

    def fwd_chunk(rows, trows, rb_slot, first, update):
        for i in heads:
            q = q_ref[0, rows, qcol(i)]
            s = lax.dot_general(q, k_ref[0, rows, qcol(i)], contract_feat,
                                preferred_element_type=F32)
            p = (s * dm_ref[i, trows, trows]).astype(BF16)
            o = jnp.dot(p, v_ref[0, rows, vcol(i)], preferred_element_type=F32)
            if not first:
                o = o + jnp.dot(q * qf_ref[i, trows, :], rf_scr[i].astype(BF16),
                                preferred_element_type=F32)
            if rb_slot is not None:
                o = o + jnp.dot(q * qb_ref[i, trows, :], rbs_scr[i, rb_slot],
                                preferred_element_type=F32)
            if update:
                state_update(rf_scr, i, rows, kf_ref[i, trows, :], first)
            mu = jnp.mean(o, axis=-1, keepdims=True)
            oc = o - mu
            var = jnp.mean(oc * oc, axis=-1, keepdims=True)
            on = oc * lax.rsqrt(var + RMS_EPS)
            hg = 0.5 * g_ref[0, rows, vcol(i)]
            y_ref[0, rows, vcol(i)] = (hg * jnp.tanh(hg) + hg) * on.astype(BF16)

    fwd_chunk(lead_rows, lead_t, lead_slot, first=True, update=True)
    for n in range(n_full):
        last = n == n_full - 1
        fwd_chunk(full_rows(n), all_t, None if last else n, first=False, update=not last)


def _retention_tables():
    c = RET_CHUNK
    log_gamma = jnp.log1p(-jnp.exp2(-5.0 - jnp.arange(RET_HEADS, dtype=F32)))
    idx = jnp.arange(c, dtype=F32)
    lg = log_gamma[:, None, None]
    dmat = jnp.exp(lg * jnp.abs(idx[:, None] - idx[None, :])[None])
    qf = jnp.exp(log_gamma[:, None] * (idx + 1.0)[None])[..., None]
    qb = jnp.exp(log_gamma[:, None] * (c - idx)[None])[..., None]
    kf = jnp.exp(log_gamma[:, None] * (c - 1.0 - idx)[None])[..., None]
    kb = jnp.exp(log_gamma[:, None] * idx[None])[..., None]
    gc = jnp.exp(log_gamma * c)
    bk = lambda x: jnp.broadcast_to(x, (RET_HEADS, c, RET_QK_DIM)).astype(BF16)
    return gc, dmat, bk(qf), bk(qb), bk(kf), bk(kb)


def _retention(p):
    b, t, _ = p.shape
    c = RET_CHUNK
    lead = t % c
    n_full = t // c
    assert lead % BF16_ROWS == 0 and lead > 0
    gc, dmat, qf, qb, kf, kb = _retention_tables()
    dk, dv, nh = RET_QK_DIM, RET_V_DIM, RET_HEADS
    hp = RET_HEADS_PER_STEP
    ngrp = nh // hp
    k_off = nh * dk // (hp * dk)
    v_off = 2 * nh * dk // (hp * dv)
    g_off = v_off + ngrp
    tab = lambda w: pl.BlockSpec((hp, c, w), lambda i, h: (h, 0, 0))
    return pl.pallas_call(
        functools.partial(_retention_kernel, n_full=n_full, lead=lead),
        grid=(b, ngrp),
        in_specs=[
            pl.BlockSpec(memory_space=pltpu.SMEM),
            pl.BlockSpec((1, t, hp * dk), lambda i, h: (i, 0, h)),
            pl.BlockSpec((1, t, hp * dk), lambda i, h: (i, 0, k_off + h)),
            pl.BlockSpec((1, t, hp * dv), lambda i, h: (i, 0, v_off + h)),
            pl.BlockSpec((1, t, hp * dv), lambda i, h: (i, 0, g_off + h)),
            tab(c), tab(dk), tab(dk), tab(dk), tab(dk),
        ],
        out_specs=pl.BlockSpec((1, t, hp * dv), lambda i, h: (i, 0, h)),
        out_shape=jax.ShapeDtypeStruct((b, t, nh * dv), BF16),
        scratch_shapes=[
            pltpu.VMEM((hp, dk, dv), F32),
            pltpu.VMEM((hp, dk, dv), F32),
            pltpu.VMEM((hp, n_full, dk, dv), BF16),
        ],
        compiler_params=_params(("arbitrary", "arbitrary")),
        name="retention",
    )(gc, p, p, p, p, dmat, qf, qb, kf, kb)


def _out_proj_kernel(y_ref, w_ref, h_ref, g_ref, o_ref, *, rt):
    for r in range(y_ref.shape[1] // rt):
        rows = pl.ds(r * rt, rt)
        m = jnp.dot(y_ref[0, rows, :], w_ref[...], preferred_element_type=F32)
        o_ref[0, rows, :] = h_ref[0, rows, :] + _rms(m, g_ref[...])


def _out_proj_first_kernel(y_ref, w_ref, x_ref, meta_ref, g_ref, o_ref):
    n_meta = meta_ref.shape[0]
    rt = o_ref.shape[1]
    res = _rms(jnp.dot(y_ref[0], w_ref[...], preferred_element_type=F32), g_ref[...])

    @pl.when(pl.program_id(1) == 0)
    def _():
        o_ref[0, 0:n_meta, :] = meta_ref[...] + res[0:n_meta, :]
        o_ref[0, n_meta:, :] = x_ref[0, 0:rt - n_meta, :] + res[n_meta:, :]

    @pl.when(pl.program_id(1) != 0)
    def _():
        o_ref[0] = x_ref[0] + res


def _out_proj_first(y, w_out, x, meta, gain):
    b, seq, d = x.shape
    n_meta = meta.shape[0]
    t = n_meta + seq
    kdim = y.shape[-1]
    rt = _row_tile(t)
    return pl.pallas_call(
        _out_proj_first_kernel,
        grid=(b, t // rt),
        in_specs=[
            pl.BlockSpec((1, rt, kdim), lambda i, r: (i, r, 0)),
            pl.BlockSpec((kdim, d), lambda i, r: (0, 0)),
            pl.BlockSpec((pl.Element(1), pl.Element(rt), pl.Element(d)),
                         lambda i, r: (i, jnp.maximum(r * (rt // SUBLANES) - n_meta // SUBLANES, 0)
                                       * SUBLANES, 0)),
            pl.BlockSpec((n_meta, d), lambda i, r: (0, 0)),
            pl.BlockSpec((1, d), lambda i, r: (0, 0)),
        ],
        out_specs=pl.BlockSpec((1, rt, d), lambda i, r: (i, r, 0)),
        out_shape=jax.ShapeDtypeStruct((b, t, d), F32),
        compiler_params=_params(("arbitrary", "arbitrary")),
        name="out_proj_first",
    )(y, w_out, x, meta, gain.reshape(1, d))


def _out_proj(y, w_out, h, gain):
    b, t, d = h.shape
    kdim = y.shape[-1]
    rt = _row_tile(t)
    window_bytes = lambda rows: 2 * (rows * kdim * 2 + 2 * rows * d * 4) + 2 * kdim * d * 2
    blk = t if window_bytes(t) <= OUT_PROJ_VMEM_BUDGET else rt
    return pl.pallas_call(
        functools.partial(_out_proj_kernel, rt=rt),
        grid=(b, t // blk),
        in_specs=[
            pl.BlockSpec((1, blk, kdim), lambda i, r: (i, r, 0)),
            pl.BlockSpec((kdim, d), lambda i, r: (0, 0)),
            pl.BlockSpec((1, blk, d), lambda i, r: (i, r, 0)),
            pl.BlockSpec((1, d), lambda i, r: (0, 0)),
        ],
        out_specs=pl.BlockSpec((1, blk, d), lambda i, r: (i, r, 0)),
        out_shape=jax.ShapeDtypeStruct((b, t, d), F32),
        compiler_params=_params(("arbitrary", "arbitrary")),
        name="out_proj",
    )(y, w_out, h, gain.reshape(1, d))


FFN_CHUNK = 1024
FFN_SUB = 256
FFN_AHEAD = 2
FFN_HALO = BF16_ROWS


def _ffn_core(wv_ref, wg_ref, cwv_ref, cwg_ref, cbv_ref, cbg_ref, wo_ref, o_ref, a_scr, uv_scr,
              ug_scr, *, rt, fc):
    halo = FFN_HALO
    a = a_scr[...]
    spc = FFN_SUB // LANES
    nsub = fc // FFN_SUB

    def up_proj(c):
        cols = slice(c * FFN_SUB, (c + 1) * FFN_SUB)
        uv = jnp.dot(a, wv_ref[:, cols], preferred_element_type=F32)
        ug = jnp.dot(a, wg_ref[:, cols], preferred_element_type=F32)
        for s in range(spc):
            uv_scr[c * spc + s] = uv[:, s * LANES:(s + 1) * LANES]
            ug_scr[c * spc + s] = ug[:, s * LANES:(s + 1) * LANES]

    def conv(u_scr, c, cw_ref, cb_ref):
        outs = []
        for s in range(spc):
            lanes = slice(c * FFN_SUB + s * LANES, c * FFN_SUB + (s + 1) * LANES)
            acc = cb_ref[:, lanes]
            for k in range(FFN_CONV_W):
                lo = halo - FFN_CONV_LEFT + k
                acc = acc + cw_ref[k:k + 1, lanes] * u_scr[c * spc + s, lo:lo + rt, :]
            outs.append(acc)
        return jnp.concatenate(outs, axis=1)

    acc = None
    for c in range(FFN_AHEAD):
        up_proj(c)
    for c in range(nsub):
        if c + FFN_AHEAD < nsub:
            up_proj(c + FFN_AHEAD)
        val = conv(uv_scr, c, cwv_ref, cbv_ref)
        gate = conv(ug_scr, c, cwg_ref, cbg_ref)
        act = (_gelu_tanh(gate) * val).astype(BF16)
        part = jnp.dot(act, wo_ref[c * FFN_SUB:(c + 1) * FFN_SUB, :], preferred_element_type=F32)
        acc = part if acc is None else acc + part
    o_ref[0] += acc


def _ffn_kernel(hp_ref, h_ref, hn_ref, g_ref, wv_ref, wg_ref, cwv_ref, cwg_ref, cbv_ref, cbg_ref,
                wo_ref, o_ref, a_scr, uv_scr, ug_scr, *, rt, nrt, nj, fc):
    r = pl.program_id(1)
    j = pl.program_id(2)
    halo = FFN_HALO

    @pl.when(j == 0)
    def _():
        g_pre = g_ref[0:1, :]
        a_scr[halo:halo + rt, :] = _rms(h_ref[0], g_pre).astype(BF16)
        prev = jnp.where(r > 0, _rms(hp_ref[0], g_pre), 0.0)
        a_scr[0:halo, :] = prev.astype(BF16)
        nxt = jnp.where(r < nrt - 1, _rms(hn_ref[0], g_pre), 0.0)
        a_scr[halo + rt:halo + rt + halo, :] = nxt.astype(BF16)
        o_ref[...] = jnp.zeros_like(o_ref)

    _ffn_core(wv_ref, wg_ref, cwv_ref, cwg_ref, cbv_ref, cbg_ref, wo_ref, o_ref, a_scr, uv_scr,
              ug_scr, rt=rt, fc=fc)

    @pl.when(j == nj - 1)
    def _():
        o_ref[0] = h_ref[0] + _rms(o_ref[0], g_ref[1:2, :])


def _ffn_last_kernel(hw_ref, g_ref, wv_ref, wg_ref, cwv_ref, cwg_ref, cbv_ref, cbg_ref,
                     wo_ref, o_ref, a_scr, uv_scr, ug_scr, *, rt, nrt, nj, fc):
    r = pl.program_id(1)
    j = pl.program_id(2)
    halo = FFN_HALO
    is_last = r == nrt - 1
    g_pre = g_ref[0:1, :]
    g_post = g_ref[1:2, :]

    @pl.when(j == 0)
    def _():
        o_ref[...] = jnp.zeros_like(o_ref)

    @pl.when(jnp.logical_and(j == 0, jnp.logical_not(is_last)))
    def _():
        a_scr[...] = _rms(hw_ref[0], g_pre).astype(BF16)

    @pl.when(jnp.logical_and(j == 0, is_last))
    def _():
        a_scr[0:rt, :] = _rms(hw_ref[0, 2 * halo:, :], g_pre).astype(BF16)
        a_scr[rt:, :] = jnp.zeros((2 * halo, a_scr.shape[1]), BF16)

    _ffn_core(wv_ref, wg_ref, cwv_ref, cwg_ref, cbv_ref, cbg_ref, wo_ref, o_ref, a_scr, uv_scr,
              ug_scr, rt=rt, fc=fc)

    @pl.when(jnp.logical_and(j == nj - 1, jnp.logical_not(is_last)))
    def _():
        o_ref[0] = hw_ref[0, halo:halo + rt, :] + _rms(o_ref[0], g_post)

    @pl.when(jnp.logical_and(j == nj - 1, is_last))
    def _():
        o_ref[0, 0:rt - halo, :] = hw_ref[0, 3 * halo:, :] + _rms(o_ref[0, 0:rt - halo, :], g_post)


def _ffn(h, gains, layer, w_in, conv_w, conv_b, w_out, drop_meta=False):
    b, t, d = h.shape
    f = w_out.shape[1]
    rt = _row_tile(t)
    nrt = t // rt
    fc = FFN_CHUNK
    nj = f // fc
    halo = FFN_HALO
    hb = rt // halo
    last = t // halo - 1
    weight_specs = [
        pl.BlockSpec((2, d), lambda i, r, j: (0, 0)),
        pl.BlockSpec((None, d, fc), lambda i, r, j: (layer, 0, j)),
        pl.BlockSpec((None, d, fc), lambda i, r, j: (layer, 0, nj + j)),
        pl.BlockSpec((None, FFN_CONV_W, fc), lambda i, r, j: (layer, 0, j)),
        pl.BlockSpec((None, FFN_CONV_W, fc), lambda i, r, j: (layer, 0, nj + j)),
        pl.BlockSpec((None, 1, fc), lambda i, r, j: (layer, 0, j)),
        pl.BlockSpec((None, 1, fc), lambda i, r, j: (layer, 0, nj + j)),
        pl.BlockSpec((None, fc, d), lambda i, r, j: (layer, j, 0)),
    ]
    conv_b = conv_b.reshape(conv_b.shape[0], 1, 2 * f)
    weights = (gains, w_in, w_in, conv_w, conv_w, conv_b, conv_b, w_out)
    if drop_meta:
        assert halo == N_META
        win = rt + 2 * halo
        max_start = (t - win) // SUBLANES
        body = _ffn_last_kernel
        act_specs = [pl.BlockSpec(
            (pl.Element(1), pl.Element(win), pl.Element(d)),
            lambda i, r, j: (i, jnp.minimum(r * (rt // SUBLANES), max_start) * SUBLANES, 0))]
        acts = (h,)
        out_rows = t - halo
    else:
        body = _ffn_kernel
        act_specs = [
            pl.BlockSpec((1, halo, d), lambda i, r, j: (i, jnp.maximum(r * hb - 1, 0), 0)),
            pl.BlockSpec((1, rt, d), lambda i, r, j: (i, r, 0)),
            pl.BlockSpec((1, halo, d), lambda i, r, j: (i, jnp.minimum((r + 1) * hb, last), 0)),
        ]
        acts = (h, h, h)
        out_rows = t
    return pl.pallas_call(
        functools.partial(body, rt=rt, nrt=nrt, nj=nj, fc=fc),
        grid=(b, nrt, nj),
        in_specs=act_specs + weight_specs,
        out_specs=pl.BlockSpec((1, rt, d), lambda i, r, j: (i, r, 0)),
        out_shape=jax.ShapeDtypeStruct((b, out_rows, d), F32),
        scratch_shapes=[
            pltpu.VMEM((rt + 2 * halo, d), BF16),
            pltpu.VMEM((fc // LANES, rt + 2 * halo, LANES), F32),
            pltpu.VMEM((fc // LANES, rt + 2 * halo, LANES), F32),
        ],
        compiler_params=_params(("arbitrary", "arbitrary", "arbitrary")),
        name="conv_ffn_last" if drop_meta else "conv_ffn",
    )(*acts, *weights)


LRU_PAD = SUBLANES
LRU_PASS1_UNROLL = 43
LRU_PASS2_UNROLL = 6


def _lru_kernel(h_ref, g_ref, wgate_ref, wx_ref, cw_ref, cb_ref, wa_ref, ba_ref, wi_ref, bi_ref,
                lam_ref, y_ref, a_scr, xp_scr, xc_scr, gate_scr, av_scr, u_scr, hs_scr,
                *, rt, nrt, t):
    nb = pl.program_id(1)
    w = LRU_BLOCK_W
    nslab = w // LANES
    seg = t // SUBLANES
    pad = LRU_PAD
    lanes = lambda s: slice(s * LANES, (s + 1) * LANES)

    @pl.when(nb == 0)
    def _():
        for r in range(nrt):
            rows = pl.ds(r * rt, rt)
            a_scr[rows, :] = _rms(h_ref[0, rows, :], g_ref[...]).astype(BF16)

    for s in range(nslab):
        xp_scr[s, 0:pad, :] = jnp.zeros((pad, LANES), F32)
        xp_scr[s, pad + t:pad + t + pad, :] = jnp.zeros((pad, LANES), F32)

    def in_proj(r):
        ext = BF16_ROWS if r + 1 < nrt else 0
        xr = jnp.dot(a_scr[pl.ds(r * rt, rt + ext), :], wx_ref[...], preferred_element_type=F32)
        for s in range(nslab):
            xp_scr[s, pl.ds(pad + r * rt, rt + ext), :] = xr[:, lanes(s)]
        gate = jnp.dot(a_scr[pl.ds(r * rt, rt), :], wgate_ref[...], preferred_element_type=F32)
        gate_scr[pl.ds(r * rt, rt), :] = _gelu_tanh(gate).astype(BF16)

    def conv(r):
        outs = []
        for s in range(nslab):
            acc = cb_ref[:, lanes(s)]
            for k in range(LRU_CONV_W):
                lo = pad + r * rt - LRU_CONV_LEFT + k
                acc = acc + cw_ref[k:k + 1, lanes(s)] * xp_scr[s, lo:lo + rt, :]
            outs.append(acc)
        xc_scr[pl.ds(r * rt, rt), :] = jnp.concatenate(outs, axis=1)

    def gate_dots(r):
        hxb = (0.5 * xc_scr[pl.ds(r * rt, rt), :]).astype(BF16)
        return [(jnp.dot(hxb, wa_ref[d, 0], preferred_element_type=F32),
                 jnp.dot(hxb, wi_ref[d, 0], preferred_element_type=F32)) for d in range(2)]

    def gate_elementwise(r, raw):
        rows = pl.ds(r * rt, rt)
        x = xc_scr[rows, :]
        hx = 0.5 * x
        for d in range(2):
            ga_half, gi_half = raw[d]
            lam = lam_ref[d:d + 1, :]
            softplus_neg = jnp.maximum(-lam, 0.0) + jnp.log1p(jnp.exp(-jnp.abs(lam)))
            c1 = (-0.5 * LRU_C) * softplus_neg
            t_a = jnp.tanh(ga_half + 0.5 * ba_ref[d:d + 1, :])
            log_a = c1 * t_a + c1
            av = jnp.exp(log_a)
            z = jnp.tanh(log_a) * (-1.0 - av * av)
            mult = jnp.where(z > 0.0, z * lax.rsqrt(z), 0.0)
            t_i = jnp.tanh(gi_half + 0.5 * bi_ref[d:d + 1, :])
            uu = mult * (hx * t_i + hx)
            for s in range(nslab):
                av_scr[d, s, rows, :] = av[:, lanes(s)]
                u_scr[d, s, rows, :] = uu[:, lanes(s)]

    in_proj(0)
    for r in range(nrt):
        conv(r)
        raw = gate_dots(r)
        if r + 1 < nrt:
            in_proj(r + 1)
        gate_elementwise(r, raw)

    sub = lax.broadcasted_iota(jnp.int32, (SUBLANES, LANES), 0)
    chains = [(d, s) for d in range(2) for s in range(nslab)]

    def strided(d, ii):
        i = ii if d == 0 else seg - 1 - ii
        return pl.ds(i, SUBLANES, stride=seg)

    def pass1(ii, carry):
        out = []
        for (d, s), (hc, pc) in zip(chains, carry):
            a_i = av_scr[d, s, strided(d, ii), :]
            out.append((a_i * hc + u_scr[d, s, strided(d, ii), :], a_i * pc))
        return tuple(out)

    init = tuple((jnp.zeros((SUBLANES, LANES), F32), jnp.ones((SUBLANES, LANES), F32))
                 for _ in chains)
    ends = lax.fori_loop(0, seg, pass1, init, unroll=LRU_PASS1_UNROLL)

    cins = []
    for (d, s), (h_end, p_end) in zip(chains, ends):
        shift = 1 if d == 0 else SUBLANES - 1
        edge = 0 if d == 0 else SUBLANES - 1
        total = h_end
        for _ in range(SUBLANES - 1):
            moved = jnp.where(sub == edge, 0.0, pltpu.roll(total, shift, 0))
            total = h_end + p_end * moved
        cins.append(jnp.where(sub == edge, 0.0, pltpu.roll(total, shift, 0)))

    def pass2(ii, carry):
        out = []
        for (d, s), hc in zip(chains, carry):
            hc = av_scr[d, s, strided(d, ii), :] * hc + u_scr[d, s, strided(d, ii), :]
            hs_scr[d, s, strided(d, ii), :] = hc
            out.append(hc)
        return tuple(out)

    lax.fori_loop(0, seg, pass2, tuple(cins), unroll=LRU_PASS2_UNROLL)

    for r in range(nrt):
        rows = pl.ds(r * rt, rt)
        gl = gate_scr[rows, :].astype(F32)
        for s in range(nslab):
            y_ref[0, rows, lanes(s)] = (
                (hs_scr[0, s, rows, :] + hs_scr[1, s, rows, :]) * gl[:, lanes(s)]).astype(BF16)


def _lru(h, gain, w_in, conv_w, conv_b, w_a, b_a, w_i, b_i, lam):
    b, t, d = h.shape
    r_width = w_in.shape[1] // 2
    w = LRU_BLOCK_W
    nblk = r_width // w
    rt = _row_tile(t)
    nrt = t // rt
    assert t % SUBLANES == 0
    nslab = w // LANES
    vec = lambda rows: pl.BlockSpec((rows, w), lambda i, n: (0, n))
    return pl.pallas_call(
        functools.partial(_lru_kernel, rt=rt, nrt=nrt, t=t),
        grid=(b, nblk),
        in_specs=[
            pl.BlockSpec((1, t, d), lambda i, n: (i, 0, 0)),
            pl.BlockSpec((1, d), lambda i, n: (0, 0)),
            pl.BlockSpec((d, w), lambda i, n: (0, n)),
            pl.BlockSpec((d, w), lambda i, n: (0, nblk + n)),
            vec(LRU_CONV_W), vec(1),
            pl.BlockSpec((2, 1, w, w), lambda i, n: (0, n, 0, 0)), vec(2),
            pl.BlockSpec((2, 1, w, w), lambda i, n: (0, n, 0, 0)), vec(2),
            vec(2),
        ],
        out_specs=pl.BlockSpec((1, t, w), lambda i, n: (i, 0, n)),
        out_shape=jax.ShapeDtypeStruct((b, t, r_width), BF16),
        scratch_shapes=[
            pltpu.VMEM((t, d), BF16),
            pltpu.VMEM((nslab, t + 2 * LRU_PAD, LANES), F32),
            pltpu.VMEM((t, w), F32),
            pltpu.VMEM((t, w), BF16),
            pltpu.VMEM((2, nslab, t, LANES), F32),
            pltpu.VMEM((2, nslab, t, LANES), F32),
            pltpu.VMEM((2, nslab, t, LANES), F32),
        ],
        compiler_params=_params(("arbitrary", "arbitrary")),
        name="rglru",
    )(h, gain.reshape(1, d), w_in, w_in, conv_w, conv_b.reshape(1, r_width), w_a, b_a, w_i, b_i, lam)


def kernel(x, meta_tokens, norm_gains, ret_w_in, ret_w_out, lru_w_in, lru_conv_w, lru_conv_b,
           lru_w_a, lru_b_a, lru_w_i, lru_b_i, lru_lambda, lru_w_out, ffn_w_in, ffn_conv_w,
           ffn_conv_b, ffn_w_out):
    meta = meta_tokens.astype(x.dtype)
    t = N_META + x.shape[1]

    pos = jnp.arange(t, dtype=F32)
    inv = ROPE_BASE ** (-jnp.arange(0, RET_QK_DIM, 2, dtype=F32) / RET_QK_DIM)
    ang = pos[:, None] * inv[None, :]
    cos, sin = jnp.cos(ang), jnp.sin(ang)

    ffn_w_in_bf = ffn_w_in.astype(BF16)
    ffn_w_out_bf = ffn_w_out.astype(BF16)

    g = norm_gains[0]
    p = _ret_in(x, meta, g[0], ret_w_in[0].astype(BF16), cos, sin)
    y = _retention(p)
    h = _out_proj_first(y, ret_w_out[0].astype(BF16), x, meta, g[1])
    h = _ffn(h, g[2:4], 0, ffn_w_in_bf, ffn_conv_w, ffn_conv_b, ffn_w_out_bf)

    g = norm_gains[1]
    y = _lru(h, g[0], lru_w_in[0].astype(BF16), lru_conv_w[0], lru_conv_b[0],
             lru_w_a[0].astype(BF16), lru_b_a[0], lru_w_i[0].astype(BF16), lru_b_i[0], lru_lambda[0])
    h = _out_proj(y, lru_w_out[0].astype(BF16), h, g[1])
    return _ffn(h, g[2:4], 1, ffn_w_in_bf, ffn_conv_w, ffn_conv_b, ffn_w_out_bf, drop_meta=True)
```

```python
import functools

import jax
import jax.numpy as jnp
from jax import lax
from jax.experimental import pallas as pl
from jax.experimental.pallas import tpu as pltpu

F32 = jnp.float32
BF16 = jnp.bfloat16

D_MODEL = 1024
N_META = 16
RMS_EPS = 1e-6

RET_HEADS = 4
RET_QK_DIM = D_MODEL // RET_HEADS
RET_V_DIM = 2 * D_MODEL // RET_HEADS
RET_CHUNK = 256
ROPE_BASE = 10000.0

LRU_BLOCKS = 4
LRU_BLOCK_W = D_MODEL // LRU_BLOCKS
LRU_C = 8.0
LRU_CONV_W = 4
LRU_CONV_LEFT = 2

FFN_CONV_W = 3
FFN_CONV_LEFT = 1

SUBLANES = 8
LANES = 128
BF16_ROWS = 16

VMEM_LIMIT_BYTES = 56 * 1024 * 1024
OUT_PROJ_VMEM_BUDGET = 48 * 1024 * 1024


def _row_tile(t):
    for n in (2, 3, 4, 5, 6, 8):
        if t % n == 0 and (t // n) % BF16_ROWS == 0:
            return t // n
    return t


def _rms(x, g):
    ms = jnp.mean(x * x, axis=-1, keepdims=True)
    return x * lax.rsqrt(ms + RMS_EPS) * g


def _gelu_tanh(x):
    c = 0.7978845608028654
    return 0.5 * x * (1.0 + jnp.tanh(c * (x + 0.044715 * (x * x * x))))


def _params(sem):
    return pltpu.CompilerParams(dimension_semantics=sem, vmem_limit_bytes=VMEM_LIMIT_BYTES)


RET_IN_CHUNK = 2 * RET_HEADS * RET_QK_DIM
RET_IN_COLS = 512


def _ret_in_kernel(x_ref, meta_ref, g_ref, w_ref, cos_ref, sin_ref, p_ref, a_scr, *, rt, nrt):
    j = pl.program_id(1)
    n_meta = meta_ref.shape[0]
    seq = x_ref.shape[1]
    xt = _row_tile(seq)

    @pl.when(j == 0)
    def _():
        a_scr[0:n_meta, :] = _rms(meta_ref[...], g_ref[...]).astype(BF16)
        for r in range(seq // xt):
            a_scr[pl.ds(n_meta + r * xt, xt), :] = _rms(
                x_ref[0, pl.ds(r * xt, xt), :], g_ref[...]).astype(BF16)

    def proj(r, cb):
        cols = slice(cb * RET_IN_COLS, (cb + 1) * RET_IN_COLS)
        return jnp.dot(a_scr[pl.ds(r * rt, rt), :], w_ref[:, cols], preferred_element_type=F32)

    half = RET_QK_DIM // 2
    ncb = RET_IN_CHUNK // RET_IN_COLS

    @pl.when(j == 0)
    def _():
        for r in range(nrt):
            rows = pl.ds(r * rt, rt)
            c = cos_ref[rows, :]
            s = sin_ref[rows, :]
            for cb in range(ncb):
                p = proj(r, cb)
                is_k = cb * RET_IN_COLS >= RET_HEADS * RET_QK_DIM
                for hh in range(RET_IN_COLS // RET_QK_DIM):
                    lo = hh * RET_QK_DIM
                    x1 = p[:, lo:lo + half]
                    x2 = p[:, lo + half:lo + RET_QK_DIM]
                    y1 = x1 * c - x2 * s
                    y2 = x2 * c + x1 * s
                    if is_k:
                        y1 = y1 * RET_QK_DIM ** -0.5
                        y2 = y2 * RET_QK_DIM ** -0.5
                    out = cb * RET_IN_COLS + lo
                    p_ref[0, rows, out:out + half] = y1.astype(BF16)
                    p_ref[0, rows, out + half:out + RET_QK_DIM] = y2.astype(BF16)

    @pl.when(j != 0)
    def _():
        for r in range(nrt):
            for cb in range(ncb):
                cols = slice(cb * RET_IN_COLS, (cb + 1) * RET_IN_COLS)
                p_ref[0, pl.ds(r * rt, rt), cols] = proj(r, cb).astype(BF16)


def _ret_in(x, meta, gain, w_in, cos, sin):
    b, seq, d = x.shape
    n_meta = meta.shape[0]
    t = n_meta + seq
    n_out = w_in.shape[1]
    rt = _row_tile(t)
    nrt = t // rt
    half = RET_QK_DIM // 2
    return pl.pallas_call(
        functools.partial(_ret_in_kernel, rt=rt, nrt=nrt),
        grid=(b, n_out // RET_IN_CHUNK),
        in_specs=[
            pl.BlockSpec((1, seq, d), lambda i, j: (i, 0, 0)),
            pl.BlockSpec((n_meta, d), lambda i, j: (0, 0)),
            pl.BlockSpec((1, d), lambda i, j: (0, 0)),
            pl.BlockSpec((d, RET_IN_CHUNK), lambda i, j: (0, j)),
            pl.BlockSpec((t, half), lambda i, j: (0, 0)),
            pl.BlockSpec((t, half), lambda i, j: (0, 0)),
        ],
        out_specs=pl.BlockSpec((1, t, RET_IN_CHUNK), lambda i, j: (i, 0, j)),
        out_shape=jax.ShapeDtypeStruct((b, t, n_out), BF16),
        scratch_shapes=[pltpu.VMEM((t, d), BF16)],
        compiler_params=_params(("arbitrary", "arbitrary")),
        name="ret_in_proj",
    )(x, meta, gain.reshape(1, d), w_in, cos, sin)


RET_HEADS_PER_STEP = 2


def _retention_kernel(gc_ref, q_ref, k_ref, v_ref, g_ref, dm_ref, qf_ref, qb_ref, kf_ref, kb_ref,
                      y_ref, rf_scr, rb_scr, rbs_scr, *, n_full, lead):
    c = RET_CHUNK
    dk, dv = RET_QK_DIM, RET_V_DIM
    heads = range(RET_HEADS_PER_STEP)
    gcs = [gc_ref[pl.program_id(1) * RET_HEADS_PER_STEP + i] for i in heads]
    contract_rows = (((0,), (0,)), ((), ()))
    contract_feat = (((1,), (1,)), ((), ()))
    qcol = lambda i: slice(i * dk, (i + 1) * dk)
    vcol = lambda i: slice(i * dv, (i + 1) * dv)

    def state_update(r_scr, i, rows, decay, first):
        kd = k_ref[0, rows, qcol(i)] * decay
        kv = lax.dot_general(kd, v_ref[0, rows, vcol(i)], contract_rows, preferred_element_type=F32)
        r_scr[i] = kv if first else gcs[i] * r_scr[i] + kv

    all_t = pl.ds(0, c)
    lead_rows = pl.ds(0, lead)
    lead_t = pl.ds(c - lead, lead)
    full_rows = lambda n: pl.ds(lead + n * c, c)
    lead_slot = n_full - 1

    for n in reversed(range(n_full)):
        first = n == n_full - 1
        if not first:
            for i in heads:
                rbs_scr[i, n] = rb_scr[i].astype(BF16)
        for i in heads:
            state_update(rb_scr, i, full_rows(n), kb_ref[i, all_t, :], first)
    for i in heads:
        rbs_scr[i, lead_slot] = rb_scr[i].astype(BF16)

    def fwd_chunk(rows, trows, rb_slot, first, update):
        qs = [q_ref[0, rows, qcol(i)] for i in heads]
        ss = [lax.dot_general(qs[i], k_ref[0, rows, qcol(i)], contract_feat,
                              preferred_element_type=F32) for i in heads]
        ps = [(ss[i] * dm_ref[i, trows, trows]).astype(BF16) for i in heads]
        os_ = []
        for i in heads:
            o = jnp.dot(ps[i], v_ref[0, rows, vcol(i)], preferred_element_type=F32)
            if not first:
                o = o + jnp.dot(qs[i] * qf_ref[i, trows, :], rf_scr[i].astype(BF16),
                                preferred_element_type=F32)
            if rb_slot is not None:
                o = o + jnp.dot(qs[i] * qb_ref[i, trows, :], rbs_scr[i, rb_slot],
                                preferred_element_type=F32)
            os_.append(o)
        if update:
            for i in heads:
                state_update(rf_scr, i, rows, kf_ref[i, trows, :], first)
        for i in heads:
            o = os_[i]
            mu = jnp.mean(o, axis=-1, keepdims=True)
            oc = o - mu
            var = jnp.mean(oc * oc, axis=-1, keepdims=True)
            on = oc * lax.rsqrt(var + RMS_EPS)
            hg = 0.5 * g_ref[0, rows, vcol(i)]
            y_ref[0, rows, vcol(i)] = (hg * jnp.tanh(hg) + hg) * on.astype(BF16)

    fwd_chunk(lead_rows, lead_t, lead_slot, first=True, update=True)
    for n in range(n_full):
        last = n == n_full - 1
        fwd_chunk(full_rows(n), all_t, None if last else n, first=False, update=not last)


def _retention_tables():
    c = RET_CHUNK
    log_gamma = jnp.log1p(-jnp.exp2(-5.0 - jnp.arange(RET_HEADS, dtype=F32)))
    idx = jnp.arange(c, dtype=F32)
    lg = log_gamma[:, None, None]
    dmat = jnp.exp(lg * jnp.abs(idx[:, None] - idx[None, :])[None])
    qf = jnp.exp(log_gamma[:, None] * (idx + 1.0)[None])[..., None]
    qb = jnp.exp(log_gamma[:, None] * (c - idx)[None])[..., None]
    kf = jnp.exp(log_gamma[:, None] * (c - 1.0 - idx)[None])[..., None]
    kb = jnp.exp(log_gamma[:, None] * idx[None])[..., None]
    gc = jnp.exp(log_gamma * c)
    bk = lambda x: jnp.broadcast_to(x, (RET_HEADS, c, RET_QK_DIM)).astype(BF16)
    return gc, dmat, bk(qf), bk(qb), bk(kf), bk(kb)


def _retention(p):
    b, t, _ = p.shape
    c = RET_CHUNK
    lead = t % c
    n_full = t // c
    assert lead % BF16_ROWS == 0 and lead > 0
    gc, dmat, qf, qb, kf, kb = _retention_tables()
    dk, dv, nh = RET_QK_DIM, RET_V_DIM, RET_HEADS
    hp = RET_HEADS_PER_STEP
    ngrp = nh // hp
    k_off = nh * dk // (hp * dk)
    v_off = 2 * nh * dk // (hp * dv)
    g_off = v_off + ngrp
    tab = lambda w: pl.BlockSpec((hp, c, w), lambda i, h: (h, 0, 0))
    return pl.pallas_call(
        functools.partial(_retention_kernel, n_full=n_full, lead=lead),
        grid=(b, ngrp),
        in_specs=[
            pl.BlockSpec(memory_space=pltpu.SMEM),
            pl.BlockSpec((1, t, hp * dk), lambda i, h: (i, 0, h)),
            pl.BlockSpec((1, t, hp * dk), lambda i, h: (i, 0, k_off + h)),
            pl.BlockSpec((1, t, hp * dv), lambda i, h: (i, 0, v_off + h)),
            pl.BlockSpec((1, t, hp * dv), lambda i, h: (i, 0, g_off + h)),
            tab(c), tab(dk), tab(dk), tab(dk), tab(dk),
        ],
        out_specs=pl.BlockSpec((1, t, hp * dv), lambda i, h: (i, 0, h)),
        out_shape=jax.ShapeDtypeStruct((b, t, nh * dv), BF16),
        scratch_shapes=[
            pltpu.VMEM((hp, dk, dv), F32),
            pltpu.VMEM((hp, dk, dv), F32),
            pltpu.VMEM((hp, n_full, dk, dv), BF16),
        ],
        compiler_params=_params(("arbitrary", "arbitrary")),
        name="retention",
    )(gc, p, p, p, p, dmat, qf, qb, kf, kb)


def _out_proj_kernel(y_ref, w_ref, h_ref, g_ref, o_ref, *, rt):
    for r in range(y_ref.shape[1] // rt):
        rows = pl.ds(r * rt, rt)
        m = jnp.dot(y_ref[0, rows, :], w_ref[...], preferred_element_type=F32)
        o_ref[0, rows, :] = h_ref[0, rows, :] + _rms(m, g_ref[...])


def _out_proj_first_kernel(y_ref, w_ref, x_ref, meta_ref, g_ref, o_ref):
    n_meta = meta_ref.shape[0]
    rt = o_ref.shape[1]
    res = _rms(jnp.dot(y_ref[0], w_ref[...], preferred_element_type=F32), g_ref[...])

    @pl.when(pl.program_id(1) == 0)
    def _():
        o_ref[0, 0:n_meta, :] = meta_ref[...] + res[0:n_meta, :]
        o_ref[0, n_meta:, :] = x_ref[0, 0:rt - n_meta, :] + res[n_meta:, :]

    @pl.when(pl.program_id(1) != 0)
    def _():
        o_ref[0] = x_ref[0] + res


def _out_proj_first(y, w_out, x, meta, gain):
    b, seq, d = x.shape
    n_meta = meta.shape[0]
    t = n_meta + seq
    kdim = y.shape[-1]
    rt = _row_tile(t)
    return pl.pallas_call(
        _out_proj_first_kernel,
        grid=(b, t // rt),
        in_specs=[
            pl.BlockSpec((1, rt, kdim), lambda i, r: (i, r, 0)),
            pl.BlockSpec((kdim, d), lambda i, r: (0, 0)),
            pl.BlockSpec((pl.Element(1), pl.Element(rt), pl.Element(d)),
                         lambda i, r: (i, jnp.maximum(r * (rt // SUBLANES) - n_meta // SUBLANES, 0)
                                       * SUBLANES, 0)),
            pl.BlockSpec((n_meta, d), lambda i, r: (0, 0)),
            pl.BlockSpec((1, d), lambda i, r: (0, 0)),
        ],
        out_specs=pl.BlockSpec((1, rt, d), lambda i, r: (i, r, 0)),
        out_shape=jax.ShapeDtypeStruct((b, t, d), F32),
        compiler_params=_params(("arbitrary", "arbitrary")),
        name="out_proj_first",
    )(y, w_out, x, meta, gain.reshape(1, d))


def _out_proj(y, w_out, h, gain):
    b, t, d = h.shape
    kdim = y.shape[-1]
    rt = _row_tile(t)
    window_bytes = lambda rows: 2 * (rows * kdim * 2 + 2 * rows * d * 4) + 2 * kdim * d * 2
    blk = t if window_bytes(t) <= OUT_PROJ_VMEM_BUDGET else rt
    return pl.pallas_call(
        functools.partial(_out_proj_kernel, rt=rt),
        grid=(b, t // blk),
        in_specs=[
            pl.BlockSpec((1, blk, kdim), lambda i, r: (i, r, 0)),
            pl.BlockSpec((kdim, d), lambda i, r: (0, 0)),
            pl.BlockSpec((1, blk, d), lambda i, r: (i, r, 0)),
            pl.BlockSpec((1, d), lambda i, r: (0, 0)),
        ],
        out_specs=pl.BlockSpec((1, blk, d), lambda i, r: (i, r, 0)),
        out_shape=jax.ShapeDtypeStruct((b, t, d), F32),
        compiler_params=_params(("arbitrary", "arbitrary")),
        name="out_proj",
    )(y, w_out, h, gain.reshape(1, d))


FFN_CHUNK = 1024
FFN_SUB = 256
FFN_AHEAD = 2
FFN_HALO = BF16_ROWS


def _ffn_core(wv_ref, wg_ref, cwv_ref, cwg_ref, cbv_ref, cbg_ref, wo_ref, o_ref, a_scr, uv_scr,
              ug_scr, *, rt, fc):
    halo = FFN_HALO
    a = a_scr[...]
    spc = FFN_SUB // LANES
    nsub = fc // FFN_SUB

    def up_proj(c):
        cols = slice(c * FFN_SUB, (c + 1) * FFN_SUB)
        uv = jnp.dot(a, wv_ref[:, cols], preferred_element_type=F32)
        ug = jnp.dot(a, wg_ref[:, cols], preferred_element_type=F32)
        for s in range(spc):
            uv_scr[c * spc + s] = uv[:, s * LANES:(s + 1) * LANES]
            ug_scr[c * spc + s] = ug[:, s * LANES:(s + 1) * LANES]

    def conv(u_scr, c, cw_ref, cb_ref):
        outs = []
        for s in range(spc):
            lanes = slice(c * FFN_SUB + s * LANES, c * FFN_SUB + (s + 1) * LANES)
            acc = cb_ref[:, lanes]
            for k in range(FFN_CONV_W):
                lo = halo - FFN_CONV_LEFT + k
                acc = acc + cw_ref[k:k + 1, lanes] * u_scr[c * spc + s, lo:lo + rt, :]
            outs.append(acc)
        return jnp.concatenate(outs, axis=1)

    acc = None
    for c in range(FFN_AHEAD):
        up_proj(c)
    for c in range(nsub):
        if c + FFN_AHEAD < nsub:
            up_proj(c + FFN_AHEAD)
        val = conv(uv_scr, c, cwv_ref, cbv_ref)
        gate = conv(ug_scr, c, cwg_ref, cbg_ref)
        act = _gelu_tanh(gate.astype(BF16)) * val.astype(BF16)
        part = jnp.dot(act, wo_ref[c * FFN_SUB:(c + 1) * FFN_SUB, :], preferred_element_type=F32)
        acc = part if acc is None else acc + part
    o_ref[0] += acc


def _ffn_kernel(hp_ref, h_ref, hn_ref, g_ref, wv_ref, wg_ref, cwv_ref, cwg_ref, cbv_ref, cbg_ref,
                wo_ref, o_ref, a_scr, uv_scr, ug_scr, *, rt, nrt, nj, fc):
    r = pl.program_id(1)
    j = pl.program_id(2)
    halo = FFN_HALO

    @pl.when(j == 0)
    def _():
        g_pre = g_ref[0:1, :]
        a_scr[halo:halo + rt, :] = _rms(h_ref[0], g_pre).astype(BF16)
        prev = jnp.where(r > 0, _rms(hp_ref[0], g_pre), 0.0)
        a_scr[0:halo, :] = prev.astype(BF16)
        nxt = jnp.where(r < nrt - 1, _rms(hn_ref[0], g_pre), 0.0)
        a_scr[halo + rt:halo + rt + halo, :] = nxt.astype(BF16)
        o_ref[...] = jnp.zeros_like(o_ref)

    _ffn_core(wv_ref, wg_ref, cwv_ref, cwg_ref, cbv_ref, cbg_ref, wo_ref, o_ref, a_scr, uv_scr,
              ug_scr, rt=rt, fc=fc)

    @pl.when(j == nj - 1)
    def _():
        o_ref[0] = h_ref[0] + _rms(o_ref[0], g_ref[1:2, :])


def _ffn_last_kernel(hw_ref, g_ref, wv_ref, wg_ref, cwv_ref, cwg_ref, cbv_ref, cbg_ref,
                     wo_ref, o_ref, a_scr, uv_scr, ug_scr, *, rt, nrt, nj, fc):
    r = pl.program_id(1)
    j = pl.program_id(2)
    halo = FFN_HALO
    is_last = r == nrt - 1
    g_pre = g_ref[0:1, :]
    g_post = g_ref[1:2, :]

    @pl.when(j == 0)
    def _():
        o_ref[...] = jnp.zeros_like(o_ref)

    @pl.when(jnp.logical_and(j == 0, jnp.logical_not(is_last)))
    def _():
        a_scr[...] = _rms(hw_ref[0], g_pre).astype(BF16)

    @pl.when(jnp.logical_and(j == 0, is_last))
    def _():
        a_scr[0:rt, :] = _rms(hw_ref[0, 2 * halo:, :], g_pre).astype(BF16)
        a_scr[rt:, :] = jnp.zeros((2 * halo, a_scr.shape[1]), BF16)

    _ffn_core(wv_ref, wg_ref, cwv_ref, cwg_ref, cbv_ref, cbg_ref, wo_ref, o_ref, a_scr, uv_scr,
              ug_scr, rt=rt, fc=fc)

    @pl.when(jnp.logical_and(j == nj - 1, jnp.logical_not(is_last)))
    def _():
        o_ref[0] = hw_ref[0, halo:halo + rt, :] + _rms(o_ref[0], g_post)

    @pl.when(jnp.logical_and(j == nj - 1, is_last))
    def _():
        o_ref[0, 0:rt - halo, :] = hw_ref[0, 3 * halo:, :] + _rms(o_ref[0, 0:rt - halo, :], g_post)


def _ffn(h, gains, layer, w_in, conv_w, conv_b, w_out, drop_meta=False):
    b, t, d = h.shape
    f = w_out.shape[1]
    rt = _row_tile(t)
    nrt = t // rt
    fc = FFN_CHUNK
    nj = f // fc
    halo = FFN_HALO
    hb = rt // halo
    last = t // halo - 1
    weight_specs = [
        pl.BlockSpec((2, d), lambda i, r, j: (0, 0)),
        pl.BlockSpec((None, d, fc), lambda i, r, j: (layer, 0, j)),
        pl.BlockSpec((None, d, fc), lambda i, r, j: (layer, 0, nj + j)),
        pl.BlockSpec((None, FFN_CONV_W, fc), lambda i, r, j: (layer, 0, j)),
        pl.BlockSpec((None, FFN_CONV_W, fc), lambda i, r, j: (layer, 0, nj + j)),
        pl.BlockSpec((None, 1, fc), lambda i, r, j: (layer, 0, j)),
        pl.BlockSpec((None, 1, fc), lambda i, r, j: (layer, 0, nj + j)),
        pl.BlockSpec((None, fc, d), lambda i, r, j: (layer, j, 0)),
    ]
    conv_b = conv_b.reshape(conv_b.shape[0], 1, 2 * f)
    weights = (gains, w_in, w_in, conv_w, conv_w, conv_b, conv_b, w_out)
    if drop_meta:
        assert halo == N_META
        win = rt + 2 * halo
        max_start = (t - win) // SUBLANES
        body = _ffn_last_kernel
        act_specs = [pl.BlockSpec(
            (pl.Element(1), pl.Element(win), pl.Element(d)),
            lambda i, r, j: (i, jnp.minimum(r * (rt // SUBLANES), max_start) * SUBLANES, 0))]
        acts = (h,)
        out_rows = t - halo
    else:
        body = _ffn_kernel
        act_specs = [
            pl.BlockSpec((1, halo, d), lambda i, r, j: (i, jnp.maximum(r * hb - 1, 0), 0)),
            pl.BlockSpec((1, rt, d), lambda i, r, j: (i, r, 0)),
            pl.BlockSpec((1, halo, d), lambda i, r, j: (i, jnp.minimum((r + 1) * hb, last), 0)),
        ]
        acts = (h, h, h)
        out_rows = t
    return pl.pallas_call(
        functools.partial(body, rt=rt, nrt=nrt, nj=nj, fc=fc),
        grid=(b, nrt, nj),
        in_specs=act_specs + weight_specs,
        out_specs=pl.BlockSpec((1, rt, d), lambda i, r, j: (i, r, 0)),
        out_shape=jax.ShapeDtypeStruct((b, out_rows, d), F32),
        scratch_shapes=[
            pltpu.VMEM((rt + 2 * halo, d), BF16),
            pltpu.VMEM((fc // LANES, rt + 2 * halo, LANES), F32),
            pltpu.VMEM((fc // LANES, rt + 2 * halo, LANES), F32),
        ],
        compiler_params=_params(("arbitrary", "arbitrary", "arbitrary")),
        name="conv_ffn_last" if drop_meta else "conv_ffn",
    )(*acts, *weights)


LRU_PAD = SUBLANES
LRU_SCAN_UNROLL = 6


def _lru_kernel(h_ref, g_ref, wgate_ref, wx_ref, cw_ref, cb_ref, wa_ref, ba_ref, wi_ref, bi_ref,
                lam_ref, y_ref, a_scr, xp_scr, xc_scr, gate_scr, av_scr, u_scr, hs_scr,
                *, rt, nrt, t):
    nb = pl.program_id(1)
    w = LRU_BLOCK_W
    nslab = w // LANES
    seg = t // SUBLANES
    pad = LRU_PAD
    lanes = lambda s: slice(s * LANES, (s + 1) * LANES)

    @pl.when(nb == 0)
    def _():
        for r in range(nrt):
            rows = pl.ds(r * rt, rt)
            a_scr[rows, :] = _rms(h_ref[0, rows, :], g_ref[...]).astype(BF16)

    for s in range(nslab):
        xp_scr[s, 0:pad, :] = jnp.zeros((pad, LANES), F32)
        xp_scr[s, pad + t:pad + t + pad, :] = jnp.zeros((pad, LANES), F32)

    def in_proj(r):
        ext = BF16_ROWS if r + 1 < nrt else 0
        xr = jnp.dot(a_scr[pl.ds(r * rt, rt + ext), :], wx_ref[...], preferred_element_type=F32)
        for s in range(nslab):
            xp_scr[s, pl.ds(pad + r * rt, rt + ext), :] = xr[:, lanes(s)]
        gate = jnp.dot(a_scr[pl.ds(r * rt, rt), :], wgate_ref[...], preferred_element_type=F32)
        gate_scr[pl.ds(r * rt, rt), :] = _gelu_tanh(gate).astype(BF16)

    def conv(r):
        outs = []
        for s in range(nslab):
            acc = cb_ref[:, lanes(s)]
            for k in range(LRU_CONV_W):
                lo = pad + r * rt - LRU_CONV_LEFT + k
                acc = acc + cw_ref[k:k + 1, lanes(s)] * xp_scr[s, lo:lo + rt, :]
            outs.append(acc)
        xc_scr[pl.ds(r * rt, rt), :] = jnp.concatenate(outs, axis=1)

    def gate_dots(r):
        hxb = (0.5 * xc_scr[pl.ds(r * rt, rt), :]).astype(BF16)
        return [(jnp.dot(hxb, wa_ref[d, 0], preferred_element_type=F32),
                 jnp.dot(hxb, wi_ref[d, 0], preferred_element_type=F32)) for d in range(2)]

    def gate_elementwise(r, raw):
        rows = pl.ds(r * rt, rt)
        x = xc_scr[rows, :]
        hx = 0.5 * x
        for d in range(2):
            ga_half, gi_half = raw[d]
            lam = lam_ref[d:d + 1, :]
            softplus_neg = jnp.maximum(-lam, 0.0) + jnp.log1p(jnp.exp(-jnp.abs(lam)))
            c1 = (-0.5 * LRU_C) * softplus_neg
            t_a = jnp.tanh(ga_half + 0.5 * ba_ref[d:d + 1, :])
            log_a = c1 * t_a + c1
            av = jnp.exp(log_a)
            z = jnp.tanh(log_a) * (-1.0 - av * av)
            mult = jnp.where(z > 0.0, z * lax.rsqrt(z), 0.0)
            t_i = jnp.tanh(gi_half + 0.5 * bi_ref[d:d + 1, :])
            uu = mult * (hx * t_i + hx)
            for s in range(nslab):
                av_scr[d, s, rows, :] = av[:, lanes(s)]
                u_scr[d, s, rows, :] = uu[:, lanes(s)]

    in_proj(0)
    for r in range(nrt):
        conv(r)
        raw = gate_dots(r)
        if r + 1 < nrt:
            in_proj(r + 1)
        gate_elementwise(r, raw)

    sub = lax.broadcasted_iota(jnp.int32, (SUBLANES, LANES), 0)
    chains = [(d, s) for d in range(2) for s in range(nslab)]

    def strided(d, ii):
        i = ii if d == 0 else seg - 1 - ii
        return pl.ds(i, SUBLANES, stride=seg)

    def pass1(ii, carry):
        out = []
        for (d, s), (hc, pc) in zip(chains, carry):
            a_i = av_scr[d, s, strided(d, ii), :]
            out.append((a_i * hc + u_scr[d, s, strided(d, ii), :], a_i * pc))
        return tuple(out)

    init = tuple((jnp.zeros((SUBLANES, LANES), F32), jnp.ones((SUBLANES, LANES), F32))
                 for _ in chains)
    ends = lax.fori_loop(0, seg, pass1, init, unroll=LRU_SCAN_UNROLL)

    cins = []
    for (d, s), (h_end, p_end) in zip(chains, ends):
        shift = 1 if d == 0 else SUBLANES - 1
        edge = 0 if d == 0 else SUBLANES - 1
        total = h_end
        for _ in range(SUBLANES - 1):
            moved = jnp.where(sub == edge, 0.0, pltpu.roll(total, shift, 0))
            total = h_end + p_end * moved
        cins.append(jnp.where(sub == edge, 0.0, pltpu.roll(total, shift, 0)))

    def pass2(ii, carry):
        out = []
        for (d, s), hc in zip(chains, carry):
            hc = av_scr[d, s, strided(d, ii), :] * hc + u_scr[d, s, strided(d, ii), :]
            hs_scr[d, s, strided(d, ii), :] = hc
            out.append(hc)
        return tuple(out)

    lax.fori_loop(0, seg, pass2, tuple(cins), unroll=LRU_SCAN_UNROLL)

    for r in range(nrt):
        rows = pl.ds(r * rt, rt)
        gl = gate_scr[rows, :].astype(F32)
        for s in range(nslab):
            y_ref[0, rows, lanes(s)] = (
                (hs_scr[0, s, rows, :] + hs_scr[1, s, rows, :]) * gl[:, lanes(s)]).astype(BF16)


def _lru(h, gain, w_in, conv_w, conv_b, w_a, b_a, w_i, b_i, lam):
    b, t, d = h.shape
    r_width = w_in.shape[1] // 2
    w = LRU_BLOCK_W
    nblk = r_width // w
    rt = _row_tile(t)
    nrt = t // rt
    assert t % SUBLANES == 0
    nslab = w // LANES
    vec = lambda rows: pl.BlockSpec((rows, w), lambda i, n: (0, n))
    return pl.pallas_call(
        functools.partial(_lru_kernel, rt=rt, nrt=nrt, t=t),
        grid=(b, nblk),
        in_specs=[
            pl.BlockSpec((1, t, d), lambda i, n: (i, 0, 0)),
            pl.BlockSpec((1, d), lambda i, n: (0, 0)),
            pl.BlockSpec((d, w), lambda i, n: (0, n)),
            pl.BlockSpec((d, w), lambda i, n: (0, nblk + n)),
            vec(LRU_CONV_W), vec(1),
            pl.BlockSpec((2, 1, w, w), lambda i, n: (0, n, 0, 0)), vec(2),
            pl.BlockSpec((2, 1, w, w), lambda i, n: (0, n, 0, 0)), vec(2),
            vec(2),
        ],
        out_specs=pl.BlockSpec((1, t, w), lambda i, n: (i, 0, n)),
        out_shape=jax.ShapeDtypeStruct((b, t, r_width), BF16),
        scratch_shapes=[
            pltpu.VMEM((t, d), BF16),
            pltpu.VMEM((nslab, t + 2 * LRU_PAD, LANES), F32),
            pltpu.VMEM((t, w), F32),
            pltpu.VMEM((t, w), BF16),
            pltpu.VMEM((2, nslab, t, LANES), F32),
            pltpu.VMEM((2, nslab, t, LANES), F32),
            pltpu.VMEM((2, nslab, t, LANES), F32),
        ],
        compiler_params=_params(("arbitrary", "arbitrary")),
        name="rglru",
    )(h, gain.reshape(1, d), w_in, w_in, conv_w, conv_b.reshape(1, r_width), w_a, b_a, w_i, b_i, lam)


def kernel(x, meta_tokens, norm_gains, ret_w_in, ret_w_out, lru_w_in, lru_conv_w, lru_conv_b,
           lru_w_a, lru_b_a, lru_w_i, lru_b_i, lru_lambda, lru_w_out, ffn_w_in, ffn_conv_w,
           ffn_conv_b, ffn_w_out):
    meta = meta_tokens.astype(x.dtype)
    t = N_META + x.shape[1]

    pos = jnp.arange(t, dtype=F32)
    inv = ROPE_BASE ** (-jnp.arange(0, RET_QK_DIM, 2, dtype=F32) / RET_QK_DIM)
    ang = pos[:, None] * inv[None, :]
    cos, sin = jnp.cos(ang), jnp.sin(ang)

    ffn_w_in_bf = ffn_w_in.astype(BF16)
    ffn_w_out_bf = ffn_w_out.astype(BF16)

    g = norm_gains[0]
    p = _ret_in(x, meta, g[0], ret_w_in[0].astype(BF16), cos, sin)
    y = _retention(p)
    h = _out_proj_first(y, ret_w_out[0].astype(BF16), x, meta, g[1])
    h = _ffn(h, g[2:4], 0, ffn_w_in_bf, ffn_conv_w, ffn_conv_b, ffn_w_out_bf)

    g = norm_gains[1]
    y = _lru(h, g[0], lru_w_in[0].astype(BF16), lru_conv_w[0], lru_conv_b[0],
             lru_w_a[0].astype(BF16), lru_b_a[0], lru_w_i[0].astype(BF16), lru_b_i[0], lru_lambda[0])
    h = _out_proj(y, lru_w_out[0].astype(BF16), h, g[1])
    return _ffn(h, g[2:4], 1, ffn_w_in_bf, ffn_conv_w, ffn_conv_b, ffn_w_out_bf, drop_meta=True)
```

```python
import functools

import jax
import jax.numpy as jnp
from jax import lax
from jax.experimental import pallas as pl
from jax.experimental.pallas import tpu as pltpu

F32 = jnp.float32
BF16 = jnp.bfloat16

D_MODEL = 1024
N_META = 16
RMS_EPS = 1e-6

RET_HEADS = 4
RET_QK_DIM = D_MODEL // RET_HEADS
RET_V_DIM = 2 * D_MODEL // RET_HEADS
RET_CHUNK = 256
ROPE_BASE = 10000.0

LRU_BLOCKS = 4
LRU_BLOCK_W = D_MODEL // LRU_BLOCKS
LRU_C = 8.0
LRU_CONV_W = 4
LRU_CONV_LEFT = 2

FFN_CONV_W = 3
FFN_CONV_LEFT = 1

SUBLANES = 8
LANES = 128
BF16_ROWS = 16

VMEM_LIMIT_BYTES = 56 * 1024 * 1024
OUT_PROJ_VMEM_BUDGET = 48 * 1024 * 1024


def _row_tile(t):
    for n in (2, 3, 4, 5, 6, 8):
        if t % n == 0 and (t // n) % BF16_ROWS == 0:
            return t // n
    return t


def _rms(x, g):
    ms = jnp.mean(x * x, axis=-1, keepdims=True)
    return x * lax.rsqrt(ms + RMS_EPS) * g


def _gelu_tanh(x):
    c = 0.7978845608028654
    return 0.5 * x * (1.0 + jnp.tanh(c * (x + 0.044715 * (x * x * x))))


def _params(sem):
    return pltpu.CompilerParams(dimension_semantics=sem, vmem_limit_bytes=VMEM_LIMIT_BYTES)


RET_IN_CHUNK = 2 * RET_HEADS * RET_QK_DIM
RET_IN_COLS = 512


def _ret_in_kernel(x_ref, meta_ref, g_ref, w_ref, cos_ref, sin_ref, p_ref, a_scr, *, rt, nrt):
    j = pl.program_id(1)
    n_meta = meta_ref.shape[0]
    seq = x_ref.shape[1]
    xt = _row_tile(seq)

    @pl.when(j == 0)
    def _():
        a_scr[0:n_meta, :] = _rms(meta_ref[...], g_ref[...]).astype(BF16)
        for r in range(seq // xt):
            a_scr[pl.ds(n_meta + r * xt, xt), :] = _rms(
                x_ref[0, pl.ds(r * xt, xt), :], g_ref[...]).astype(BF16)

    def proj(r, cb):
        cols = slice(cb * RET_IN_COLS, (cb + 1) * RET_IN_COLS)
        return jnp.dot(a_scr[pl.ds(r * rt, rt), :], w_ref[:, cols], preferred_element_type=F32)

    half = RET_QK_DIM // 2
    ncb = RET_IN_CHUNK // RET_IN_COLS

    @pl.when(j == 0)
    def _():
        for r in range(nrt):
            rows = pl.ds(r * rt, rt)
            c = cos_ref[rows, :]
            s = sin_ref[rows, :]
            for cb in range(ncb):
                p = proj(r, cb)
                is_k = cb * RET_IN_COLS >= RET_HEADS * RET_QK_DIM
                for hh in range(RET_IN_COLS // RET_QK_DIM):
                    lo = hh * RET_QK_DIM
                    x1 = p[:, lo:lo + half]
                    x2 = p[:, lo + half:lo + RET_QK_DIM]
                    y1 = x1 * c - x2 * s
                    y2 = x2 * c + x1 * s
                    if is_k:
                        y1 = y1 * RET_QK_DIM ** -0.5
                        y2 = y2 * RET_QK_DIM ** -0.5
                    out = cb * RET_IN_COLS + lo
                    p_ref[0, rows, out:out + half] = y1.astype(BF16)
                    p_ref[0, rows, out + half:out + RET_QK_DIM] = y2.astype(BF16)

    @pl.when(j != 0)
    def _():
        for r in range(nrt):
            for cb in range(ncb):
                cols = slice(cb * RET_IN_COLS, (cb + 1) * RET_IN_COLS)
                p_ref[0, pl.ds(r * rt, rt), cols] = proj(r, cb).astype(BF16)


def _ret_in(x, meta, gain, w_in, cos, sin):
    b, seq, d = x.shape
    n_meta = meta.shape[0]
    t = n_meta + seq
    n_out = w_in.shape[1]
    rt = _row_tile(t)
    nrt = t // rt
    half = RET_QK_DIM // 2
    return pl.pallas_call(
        functools.partial(_ret_in_kernel, rt=rt, nrt=nrt),
        grid=(b, n_out // RET_IN_CHUNK),
        in_specs=[
            pl.BlockSpec((1, seq, d), lambda i, j: (i, 0, 0)),
            pl.BlockSpec((n_meta, d), lambda i, j: (0, 0)),
            pl.BlockSpec((1, d), lambda i, j: (0, 0)),
            pl.BlockSpec((d, RET_IN_CHUNK), lambda i, j: (0, j)),
            pl.BlockSpec((t, half), lambda i, j: (0, 0)),
            pl.BlockSpec((t, half), lambda i, j: (0, 0)),
        ],
        out_specs=pl.BlockSpec((1, t, RET_IN_CHUNK), lambda i, j: (i, 0, j)),
        out_shape=jax.ShapeDtypeStruct((b, t, n_out), BF16),
        scratch_shapes=[pltpu.VMEM((t, d), BF16)],
        compiler_params=_params(("arbitrary", "arbitrary")),
        name="ret_in_proj",
    )(x, meta, gain.reshape(1, d), w_in, cos, sin)


RET_HEADS_PER_STEP = 2


def _retention_kernel(gc_ref, q_ref, k_ref, v_ref, g_ref, dm_ref, qf_ref, qb_ref, kf_ref, kb_ref,
                      y_ref, rf_scr, rb_scr, rbs_scr, *, n_full, lead):
    c = RET_CHUNK
    dk, dv = RET_QK_DIM, RET_V_DIM
    heads = range(RET_HEADS_PER_STEP)
    gcs = [gc_ref[pl.program_id(1) * RET_HEADS_PER_STEP + i] for i in heads]
    contract_rows = (((0,), (0,)), ((), ()))
    contract_feat = (((1,), (1,)), ((), ()))
    qcol = lambda i: slice(i * dk, (i + 1) * dk)
    vcol = lambda i: slice(i * dv, (i + 1) * dv)

    def state_update(r_scr, i, rows, decay, first):
        kd = k_ref[0, rows, qcol(i)] * decay
        kv = lax.dot_general(kd, v_ref[0, rows, vcol(i)], contract_rows, preferred_element_type=F32)
        r_scr[i] = kv if first else gcs[i] * r_scr[i] + kv

    all_t = pl.ds(0, c)
    lead_rows = pl.ds(0, lead)
    lead_t = pl.ds(c - lead, lead)
    full_rows = lambda n: pl.ds(lead + n * c, c)
    lead_slot = n_full - 1

    for n in reversed(range(n_full)):
        first = n == n_full - 1
        if not first:
            for i in heads:
                rbs_scr[i, n] = rb_scr[i].astype(BF16)
        for i in heads:
            state_update(rb_scr, i, full_rows(n), kb_ref[i, all_t, :], first)
    for i in heads:
        rbs_scr[i, lead_slot] = rb_scr[i].astype(BF16)

    def fwd_chunk(rows, trows, rb_slot, first, update):
        for i in heads:
            q = q_ref[0, rows, qcol(i)]
            s = lax.dot_general(q, k_ref[0, rows, qcol(i)], contract_feat,
                                preferred_element_type=F32)
            p = (s * dm_ref[i, trows, trows]).astype(BF16)
            o = jnp.dot(p, v_ref[0, rows, vcol(i)], preferred_element_type=F32)
            if not first:
                o = o + jnp.dot(q * qf_ref[i, trows, :], rf_scr[i].astype(BF16),
                                preferred_element_type=F32)
            if rb_slot is not None:
                o = o + jnp.dot(q * qb_ref[i, trows, :], rbs_scr[i, rb_slot],
                                preferred_element_type=F32)
            if update:
                state_update(rf_scr, i, rows, kf_ref[i, trows, :], first)
            mu = jnp.mean(o, axis=-1, keepdims=True)
            oc = o - mu
            var = jnp.mean(oc * oc, axis=-1, keepdims=True)
            on = oc * lax.rsqrt(var + RMS_EPS)
            hg = 0.5 * g_ref[0, rows, vcol(i)]
            y_ref[0, rows, vcol(i)] = (hg * jnp.tanh(hg) + hg) * on.astype(BF16)

    fwd_chunk(lead_rows, lead_t, lead_slot, first=True, update=True)
    for n in range(n_full):
        last = n == n_full - 1
        fwd_chunk(full_rows(n), all_t, None if last else n, first=False, update=not last)


def _retention_tables():
    c = RET_CHUNK
    log_gamma = jnp.log1p(-jnp.exp2(-5.0 - jnp.arange(RET_HEADS, dtype=F32)))
    idx = jnp.arange(c, dtype=F32)
    lg = log_gamma[:, None, None]
    dmat = jnp.exp(lg * jnp.abs(idx[:, None] - idx[None, :])[None])
    qf = jnp.exp(log_gamma[:, None] * (idx + 1.0)[None])[..., None]
    qb = jnp.exp(log_gamma[:, None] * (c - idx)[None])[..., None]
    kf = jnp.exp(log_gamma[:, None] * (c - 1.0 - idx)[None])[..., None]
    kb = jnp.exp(log_gamma[:, None] * idx[None])[..., None]
    gc = jnp.exp(log_gamma * c)
    bk = lambda x: jnp.broadcast_to(x, (RET_HEADS, c, RET_QK_DIM)).astype(BF16)
    return gc, dmat, bk(qf), bk(qb), bk(kf), bk(kb)


def _retention(p):
    b, t, _ = p.shape
    c = RET_CHUNK
    lead = t % c
    n_full = t // c
    assert lead % BF16_ROWS == 0 and lead > 0
    gc, dmat, qf, qb, kf, kb = _retention_tables()
    dk, dv, nh = RET_QK_DIM, RET_V_DIM, RET_HEADS
    hp = RET_HEADS_PER_STEP
    ngrp = nh // hp
    k_off = nh * dk // (hp * dk)
    v_off = 2 * nh * dk // (hp * dv)
    g_off = v_off + ngrp
    tab = lambda w: pl.BlockSpec((hp, c, w), lambda i, h: (h, 0, 0))
    return pl.pallas_call(
        functools.partial(_retention_kernel, n_full=n_full, lead=lead),
        grid=(b, ngrp),
        in_specs=[
            pl.BlockSpec(memory_space=pltpu.SMEM),
            pl.BlockSpec((1, t, hp * dk), lambda i, h: (i, 0, h)),
            pl.BlockSpec((1, t, hp * dk), lambda i, h: (i, 0, k_off + h)),
            pl.BlockSpec((1, t, hp * dv), lambda i, h: (i, 0, v_off + h)),
            pl.BlockSpec((1, t, hp * dv), lambda i, h: (i, 0, g_off + h)),
            tab(c), tab(dk), tab(dk), tab(dk), tab(dk),
        ],
        out_specs=pl.BlockSpec((1, t, hp * dv), lambda i, h: (i, 0, h)),
        out_shape=jax.ShapeDtypeStruct((b, t, nh * dv), BF16),
        scratch_shapes=[
            pltpu.VMEM((hp, dk, dv), F32),
            pltpu.VMEM((hp, dk, dv), F32),
            pltpu.VMEM((hp, n_full, dk, dv), BF16),
        ],
        compiler_params=_params(("arbitrary", "arbitrary")),
        name="retention",
    )(gc, p, p, p, p, dmat, qf, qb, kf, kb)


def _out_proj_kernel(y_ref, w_ref, h_ref, g_ref, o_ref, *, rt):
    for r in range(y_ref.shape[1] // rt):
        rows = pl.ds(r * rt, rt)
        m = jnp.dot(y_ref[0, rows, :], w_ref[...], preferred_element_type=F32)
        o_ref[0, rows, :] = h_ref[0, rows, :] + _rms(m, g_ref[...])


def _out_proj_first_kernel(y_ref, w_ref, x_ref, meta_ref, g_ref, o_ref):
    n_meta = meta_ref.shape[0]
    rt = o_ref.shape[1]
    res = _rms(jnp.dot(y_ref[0], w_ref[...], preferred_element_type=F32), g_ref[...])

    @pl.when(pl.program_id(1) == 0)
    def _():
        o_ref[0, 0:n_meta, :] = meta_ref[...] + res[0:n_meta, :]
        o_ref[0, n_meta:, :] = x_ref[0, 0:rt - n_meta, :] + res[n_meta:, :]

    @pl.when(pl.program_id(1) != 0)
    def _():
        o_ref[0] = x_ref[0] + res


def _out_proj_first(y, w_out, x, meta, gain):
    b, seq, d = x.shape
    n_meta = meta.shape[0]
    t = n_meta + seq
    kdim = y.shape[-1]
    rt = _row_tile(t)
    return pl.pallas_call(
        _out_proj_first_kernel,
        grid=(b, t // rt),
        in_specs=[
            pl.BlockSpec((1, rt, kdim), lambda i, r: (i, r, 0)),
            pl.BlockSpec((kdim, d), lambda i, r: (0, 0)),
            pl.BlockSpec((pl.Element(1), pl.Element(rt), pl.Element(d)),
                         lambda i, r: (i, jnp.maximum(r * (rt // SUBLANES) - n_meta // SUBLANES, 0)
                                       * SUBLANES, 0)),
            pl.BlockSpec((n_meta, d), lambda i, r: (0, 0)),
            pl.BlockSpec((1, d), lambda i, r: (0, 0)),
        ],
        out_specs=pl.BlockSpec((1, rt, d), lambda i, r: (i, r, 0)),
        out_shape=jax.ShapeDtypeStruct((b, t, d), F32),
        compiler_params=_params(("arbitrary", "arbitrary")),
        name="out_proj_first",
    )(y, w_out, x, meta, gain.reshape(1, d))


def _out_proj(y, w_out, h, gain):
    b, t, d = h.shape
    kdim = y.shape[-1]
    rt = _row_tile(t)
    window_bytes = lambda rows: 2 * (rows * kdim * 2 + 2 * rows * d * 4) + 2 * kdim * d * 2
    blk = t if window_bytes(t) <= OUT_PROJ_VMEM_BUDGET else rt
    return pl.pallas_call(
        functools.partial(_out_proj_kernel, rt=rt),
        grid=(b, t // blk),
        in_specs=[
            pl.BlockSpec((1, blk, kdim), lambda i, r: (i, r, 0)),
            pl.BlockSpec((kdim, d), lambda i, r: (0, 0)),
            pl.BlockSpec((1, blk, d), lambda i, r: (i, r, 0)),
            pl.BlockSpec((1, d), lambda i, r: (0, 0)),
        ],
        out_specs=pl.BlockSpec((1, blk, d), lambda i, r: (i, r, 0)),
        out_shape=jax.ShapeDtypeStruct((b, t, d), F32),
        compiler_params=_params(("arbitrary", "arbitrary")),
        name="out_proj",
    )(y, w_out, h, gain.reshape(1, d))


FFN_CHUNK = 1024
FFN_SUB = 256
FFN_AHEAD = 2
FFN_HALO = BF16_ROWS


def _ffn_core(wv_ref, wg_ref, cwv_ref, cwg_ref, cbv_ref, cbg_ref, wo_ref, o_ref, a_scr, uv_scr,
              ug_scr, *, rt, fc):
    halo = FFN_HALO
    a = a_scr[...]
    spc = FFN_SUB // LANES
    nsub = fc // FFN_SUB

    def up_proj(c):
        cols = slice(c * FFN_SUB, (c + 1) * FFN_SUB)
        uv = jnp.dot(a, wv_ref[:, cols], preferred_element_type=F32)
        ug = jnp.dot(a, wg_ref[:, cols], preferred_element_type=F32)
        for s in range(spc):
            uv_scr[c * spc + s] = uv[:, s * LANES:(s + 1) * LANES]
            ug_scr[c * spc + s] = ug[:, s * LANES:(s + 1) * LANES]

    def conv(u_scr, c, cw_ref, cb_ref):
        outs = []
        for s in range(spc):
            lanes = slice(c * FFN_SUB + s * LANES, c * FFN_SUB + (s + 1) * LANES)
            acc = cb_ref[:, lanes]
            for k in range(FFN_CONV_W):
                lo = halo - FFN_CONV_LEFT + k
                acc = acc + cw_ref[k:k + 1, lanes] * u_scr[c * spc + s, lo:lo + rt, :]
            outs.append(acc)
        return jnp.concatenate(outs, axis=1)

    acc = None
    for c in range(FFN_AHEAD):
        up_proj(c)
    for c in range(nsub):
        if c + FFN_AHEAD < nsub:
            up_proj(c + FFN_AHEAD)
        val = conv(uv_scr, c, cwv_ref, cbv_ref)
        gate = conv(ug_scr, c, cwg_ref, cbg_ref)
        act = _gelu_tanh(gate.astype(BF16)) * val.astype(BF16)
        part = jnp.dot(act, wo_ref[c * FFN_SUB:(c + 1) * FFN_SUB, :], preferred_element_type=F32)
        acc = part if acc is None else acc + part
    o_ref[0] += acc


def _ffn_kernel(hp_ref, h_ref, hn_ref, g_ref, wv_ref, wg_ref, cwv_ref, cwg_ref, cbv_ref, cbg_ref,
                wo_ref, o_ref, a_scr, uv_scr, ug_scr, *, rt, nrt, nj, fc):
    r = pl.program_id(1)
    j = pl.program_id(2)
    halo = FFN_HALO

    @pl.when(j == 0)
    def _():
        g_pre = g_ref[0:1, :]
        a_scr[halo:halo + rt, :] = _rms(h_ref[0], g_pre).astype(BF16)
        prev = jnp.where(r > 0, _rms(hp_ref[0], g_pre), 0.0)
        a_scr[0:halo, :] = prev.astype(BF16)
        nxt = jnp.where(r < nrt - 1, _rms(hn_ref[0], g_pre), 0.0)
        a_scr[halo + rt:halo + rt + halo, :] = nxt.astype(BF16)
        o_ref[...] = jnp.zeros_like(o_ref)

    _ffn_core(wv_ref, wg_ref, cwv_ref, cwg_ref, cbv_ref, cbg_ref, wo_ref, o_ref, a_scr, uv_scr,
              ug_scr, rt=rt, fc=fc)

    @pl.when(j == nj - 1)
    def _():
        o_ref[0] = h_ref[0] + _rms(o_ref[0], g_ref[1:2, :])


def _ffn_last_kernel(hw_ref, g_ref, wv_ref, wg_ref, cwv_ref, cwg_ref, cbv_ref, cbg_ref,
                     wo_ref, o_ref, a_scr, uv_scr, ug_scr, *, rt, nrt, nj, fc):
    r = pl.program_id(1)
    j = pl.program_id(2)
    halo = FFN_HALO
    is_last = r == nrt - 1
    g_pre = g_ref[0:1, :]
    g_post = g_ref[1:2, :]

    @pl.when(j == 0)
    def _():
        o_ref[...] = jnp.zeros_like(o_ref)

    @pl.when(jnp.logical_and(j == 0, jnp.logical_not(is_last)))
    def _():
        a_scr[...] = _rms(hw_ref[0], g_pre).astype(BF16)

    @pl.when(jnp.logical_and(j == 0, is_last))
    def _():
        a_scr[0:rt, :] = _rms(hw_ref[0, 2 * halo:, :], g_pre).astype(BF16)
        a_scr[rt:, :] = jnp.zeros((2 * halo, a_scr.shape[1]), BF16)

    _ffn_core(wv_ref, wg_ref, cwv_ref, cwg_ref, cbv_ref, cbg_ref, wo_ref, o_ref, a_scr, uv_scr,
              ug_scr, rt=rt, fc=fc)

    @pl.when(jnp.logical_and(j == nj - 1, jnp.logical_not(is_last)))
    def _():
        o_ref[0] = hw_ref[0, halo:halo + rt, :] + _rms(o_ref[0], g_post)

    @pl.when(jnp.logical_and(j == nj - 1, is_last))
    def _():
        o_ref[0, 0:rt - halo, :] = hw_ref[0, 3 * halo:, :] + _rms(o_ref[0, 0:rt - halo, :], g_post)


def _ffn(h, gains, layer, w_in, conv_w, conv_b, w_out, drop_meta=False):
    b, t, d = h.shape
    f = w_out.shape[1]
    rt = _row_tile(t)
    nrt = t // rt
    fc = FFN_CHUNK
    nj = f // fc
    halo = FFN_HALO
    hb = rt // halo
    last = t // halo - 1
    weight_specs = [
        pl.BlockSpec((2, d), lambda i, r, j: (0, 0)),
        pl.BlockSpec((None, d, fc), lambda i, r, j: (layer, 0, j)),
        pl.BlockSpec((None, d, fc), lambda i, r, j: (layer, 0, nj + j)),
        pl.BlockSpec((None, FFN_CONV_W, fc), lambda i, r, j: (layer, 0, j)),
        pl.BlockSpec((None, FFN_CONV_W, fc), lambda i, r, j: (layer, 0, nj + j)),
        pl.BlockSpec((None, 1, fc), lambda i, r, j: (layer, 0, j)),
        pl.BlockSpec((None, 1, fc), lambda i, r, j: (layer, 0, nj + j)),
        pl.BlockSpec((None, fc, d), lambda i, r, j: (layer, j, 0)),
    ]
    conv_b = conv_b.reshape(conv_b.shape[0], 1, 2 * f)
    weights = (gains, w_in, w_in, conv_w, conv_w, conv_b, conv_b, w_out)
    if drop_meta:
        assert halo == N_META
        win = rt + 2 * halo
        max_start = (t - win) // SUBLANES
        body = _ffn_last_kernel
        act_specs = [pl.BlockSpec(
            (pl.Element(1), pl.Element(win), pl.Element(d)),
            lambda i, r, j: (i, jnp.minimum(r * (rt // SUBLANES), max_start) * SUBLANES, 0))]
        acts = (h,)
        out_rows = t - halo
    else:
        body = _ffn_kernel
        act_specs = [
            pl.BlockSpec((1, halo, d), lambda i, r, j: (i, jnp.maximum(r * hb - 1, 0), 0)),
            pl.BlockSpec((1, rt, d), lambda i, r, j: (i, r, 0)),
            pl.BlockSpec((1, halo, d), lambda i, r, j: (i, jnp.minimum((r + 1) * hb, last), 0)),
        ]
        acts = (h, h, h)
        out_rows = t
    return pl.pallas_call(
        functools.partial(body, rt=rt, nrt=nrt, nj=nj, fc=fc),
        grid=(b, nrt, nj),
        in_specs=act_specs + weight_specs,
        out_specs=pl.BlockSpec((1, rt, d), lambda i, r, j: (i, r, 0)),
        out_shape=jax.ShapeDtypeStruct((b, out_rows, d), F32),
        scratch_shapes=[
            pltpu.VMEM((rt + 2 * halo, d), BF16),
            pltpu.VMEM((fc // LANES, rt + 2 * halo, LANES), F32),
            pltpu.VMEM((fc // LANES, rt + 2 * halo, LANES), F32),
        ],
        compiler_params=_params(("arbitrary", "arbitrary", "arbitrary")),
        name="conv_ffn_last" if drop_meta else "conv_ffn",
    )(*acts, *weights)


LRU_PAD = SUBLANES
LRU_PASS1_UNROLL = 43
LRU_PASS2_UNROLL = 6


def _lru_kernel(h_ref, g_ref, wgate_ref, wx_ref, cw_ref, cb_ref, wa_ref, ba_ref, wi_ref, bi_ref,
                lam_ref, y_ref, a_scr, xp_scr, xc_scr, gate_scr, av_scr, u_scr, hs_scr,
                *, rt, nrt, t):
    nb = pl.program_id(1)
    w = LRU_BLOCK_W
    nslab = w // LANES
    seg = t // SUBLANES
    pad = LRU_PAD
    lanes = lambda s: slice(s * LANES, (s + 1) * LANES)

    @pl.when(nb == 0)
    def _():
        for r in range(nrt):
            rows = pl.ds(r * rt, rt)
            a_scr[rows, :] = _rms(h_ref[0, rows, :], g_ref[...]).astype(BF16)

    for s in range(nslab):
        xp_scr[s, 0:pad, :] = jnp.zeros((pad, LANES), F32)
        xp_scr[s, pad + t:pad + t + pad, :] = jnp.zeros((pad, LANES), F32)

    def in_proj(r):
        ext = BF16_ROWS if r + 1 < nrt else 0
        xr = jnp.dot(a_scr[pl.ds(r * rt, rt + ext), :], wx_ref[...], preferred_element_type=F32)
        for s in range(nslab):
            xp_scr[s, pl.ds(pad + r * rt, rt + ext), :] = xr[:, lanes(s)]
        gate = jnp.dot(a_scr[pl.ds(r * rt, rt), :], wgate_ref[...], preferred_element_type=F32)
        gate_scr[pl.ds(r * rt, rt), :] = _gelu_tanh(gate).astype(BF16)

    def conv(r):
        outs = []
        for s in range(nslab):
            acc = cb_ref[:, lanes(s)]
            for k in range(LRU_CONV_W):
                lo = pad + r * rt - LRU_CONV_LEFT + k
                acc = acc + cw_ref[k:k + 1, lanes(s)] * xp_scr[s, lo:lo + rt, :]
            outs.append(acc)
        xc_scr[pl.ds(r * rt, rt), :] = jnp.concatenate(outs, axis=1)

    def gate_dots(r):
        hxb = (0.5 * xc_scr[pl.ds(r * rt, rt), :]).astype(BF16)
        return [(jnp.dot(hxb, wa_ref[d, 0], preferred_element_type=F32),
                 jnp.dot(hxb, wi_ref[d, 0], preferred_element_type=F32)) for d in range(2)]

    def gate_elementwise(r, raw):
        rows = pl.ds(r * rt, rt)
        x = xc_scr[rows, :]
        hx = 0.5 * x
        for d in range(2):
            ga_half, gi_half = raw[d]
            lam = lam_ref[d:d + 1, :]
            softplus_neg = jnp.maximum(-lam, 0.0) + jnp.log1p(jnp.exp(-jnp.abs(lam)))
            c1 = (-0.5 * LRU_C) * softplus_neg
            t_a = jnp.tanh(ga_half + 0.5 * ba_ref[d:d + 1, :])
            log_a = c1 * t_a + c1
            av = jnp.exp(log_a)
            z = jnp.tanh(log_a) * (-1.0 - av * av)
            mult = jnp.where(z > 0.0, z * lax.rsqrt(z), 0.0)
            t_i = jnp.tanh(gi_half + 0.5 * bi_ref[d:d + 1, :])
            uu = mult * (hx * t_i + hx)
            for s in range(nslab):
                av_scr[d, s, rows, :] = av[:, lanes(s)]
                u_scr[d, s, rows, :] = uu[:, lanes(s)]

    in_proj(0)
    for r in range(nrt):
        conv(r)
        raw = gate_dots(r)
        if r + 1 < nrt:
            in_proj(r + 1)
        gate_elementwise(r, raw)

    sub = lax.broadcasted_iota(jnp.int32, (SUBLANES, LANES), 0)
    chains = [(d, s) for d in range(2) for s in range(nslab)]

    def strided(d, ii):
        i = ii if d == 0 else seg - 1 - ii
        return pl.ds(i, SUBLANES, stride=seg)

    def pass1(ii, carry):
        out = []
        for (d, s), (hc, pc) in zip(chains, carry):
            a_i = av_scr[d, s, strided(d, ii), :]
            out.append((a_i * hc + u_scr[d, s, strided(d, ii), :], a_i * pc))
        return tuple(out)

    init = tuple((jnp.zeros((SUBLANES, LANES), F32), jnp.ones((SUBLANES, LANES), F32))
                 for _ in chains)
    ends = lax.fori_loop(0, seg, pass1, init, unroll=LRU_PASS1_UNROLL)

    cins = []
    for (d, s), (h_end, p_end) in zip(chains, ends):
        shift = 1 if d == 0 else SUBLANES - 1
        edge = 0 if d == 0 else SUBLANES - 1
        total = h_end
        for _ in range(SUBLANES - 1):
            moved = jnp.where(sub == edge, 0.0, pltpu.roll(total, shift, 0))
            total = h_end + p_end * moved
        cins.append(jnp.where(sub == edge, 0.0, pltpu.roll(total, shift, 0)))

    def pass2(ii, carry):
        out = []
        for (d, s), hc in zip(chains, carry):
            hc = av_scr[d, s, strided(d, ii), :] * hc + u_scr[d, s, strided(d, ii), :]
            hs_scr[d, s, strided(d, ii), :] = hc
            out.append(hc)
        return tuple(out)

    lax.fori_loop(0, seg, pass2, tuple(cins), unroll=LRU_PASS2_UNROLL)

    for r in range(nrt):
        rows = pl.ds(r * rt, rt)
        gl = gate_scr[rows, :]
        for s in range(nslab):
            y_ref[0, rows, lanes(s)] = (
                (hs_scr[0, s, rows, :] + hs_scr[1, s, rows, :]).astype(BF16) * gl[:, lanes(s)])


def _lru(h, gain, w_in, conv_w, conv_b, w_a, b_a, w_i, b_i, lam):
    b, t, d = h.shape
    r_width = w_in.shape[1] // 2
    w = LRU_BLOCK_W
    nblk = r_width // w
    rt = _row_tile(t)
    nrt = t // rt
    assert t % SUBLANES == 0
    nslab = w // LANES
    vec = lambda rows: pl.BlockSpec((rows, w), lambda i, n: (0, n))
    return pl.pallas_call(
        functools.partial(_lru_kernel, rt=rt, nrt=nrt, t=t),
        grid=(b, nblk),
        in_specs=[
            pl.BlockSpec((1, t, d), lambda i, n: (i, 0, 0)),
            pl.BlockSpec((1, d), lambda i, n: (0, 0)),
            pl.BlockSpec((d, w), lambda i, n: (0, n)),
            pl.BlockSpec((d, w), lambda i, n: (0, nblk + n)),
            vec(LRU_CONV_W), vec(1),
            pl.BlockSpec((2, 1, w, w), lambda i, n: (0, n, 0, 0)), vec(2),
            pl.BlockSpec((2, 1, w, w), lambda i, n: (0, n, 0, 0)), vec(2),
            vec(2),
        ],
        out_specs=pl.BlockSpec((1, t, w), lambda i, n: (i, 0, n)),
        out_shape=jax.ShapeDtypeStruct((b, t, r_width), BF16),
        scratch_shapes=[
            pltpu.VMEM((t, d), BF16),
            pltpu.VMEM((nslab, t + 2 * LRU_PAD, LANES), F32),
            pltpu.VMEM((t, w), F32),
            pltpu.VMEM((t, w), BF16),
            pltpu.VMEM((2, nslab, t, LANES), F32),
            pltpu.VMEM((2, nslab, t, LANES), F32),
            pltpu.VMEM((2, nslab, t, LANES), F32),
        ],
        compiler_params=_params(("arbitrary", "arbitrary")),
        name="rglru",
    )(h, gain.reshape(1, d), w_in, w_in, conv_w, conv_b.reshape(1, r_width), w_a, b_a, w_i, b_i, lam)


def kernel(x, meta_tokens, norm_gains, ret_w_in, ret_w_out, lru_w_in, lru_conv_w, lru_conv_b,
           lru_w_a, lru_b_a, lru_w_i, lru_b_i, lru_lambda, lru_w_out, ffn_w_in, ffn_conv_w,
           ffn_conv_b, ffn_w_out):
    meta = meta_tokens.astype(x.dtype)
    t = N_META + x.shape[1]

    pos = jnp.arange(t, dtype=F32)
    inv = ROPE_BASE ** (-jnp.arange(0, RET_QK_DIM, 2, dtype=F32) / RET_QK_DIM)
    ang = pos[:, None] * inv[None, :]
    cos, sin = jnp.cos(ang), jnp.sin(ang)

    ffn_w_in_bf = ffn_w_in.astype(BF16)
    ffn_w_out_bf = ffn_w_out.astype(BF16)

    g = norm_gains[0]
    p = _ret_in(x, meta, g[0], ret_w_in[0].astype(BF16), cos, sin)
    y = _retention(p)
    h = _out_proj_first(y, ret_w_out[0].astype(BF16), x, meta, g[1])
    h = _ffn(h, g[2:4], 0, ffn_w_in_bf, ffn_conv_w, ffn_conv_b, ffn_w_out_bf)

    g = norm_gains[1]
    y = _lru(h, g[0], lru_w_in[0].astype(BF16), lru_conv_w[0], lru_conv_b[0],
             lru_w_a[0].astype(BF16), lru_b_a[0], lru_w_i[0].astype(BF16), lru_b_i[0], lru_lambda[0])
    h = _out_proj(y, lru_w_out[0].astype(BF16), h, g[1])
    return _ffn(h, g[2:4], 1, ffn_w_in_bf, ffn_conv_w, ffn_conv_b, ffn_w_out_bf, drop_meta=True)
```
